```python
import jax, jax.numpy as jnp
from jax import lax
import numpy as np

D_MODEL = 1024
BATCH = 1
SEQ = 16384
DEPTH = 4

N_A = DEPTH // 2
N_B = DEPTH - N_A
EPS = 1e-6

NH_A = 4
DV_A = D_MODEL // NH_A
DQK_A = DV_A // 2
CHUNK = 64
GATE_CAP = 15.0
QK_W = NH_A * DQK_A
V_W = NH_A * DV_A
IN_A = 2 * QK_W + 2 * V_W + 2 * NH_A

NH_B = 16
KVH_B = 2
GRP_B = NH_B // KVH_B
DH_B = 64
WINDOW = 128
BLK = WINDOW

D_FF = ((8 * D_MODEL // 3 + 255) // 256) * 256

kernel_name = "yoco_mlstm_swa_sink_hybrid"


def rmsnorm(x, g):
    xf = x.astype(jnp.float32)
    y = xf * lax.rsqrt(jnp.mean(xf * xf, axis=-1, keepdims=True) + EPS)
    return (y * g.astype(jnp.float32)).astype(x.dtype)


def swiglu(x, w_gate_up, w_down):
    gu = x @ w_gate_up
    g, u = jnp.split(gu, 2, axis=-1)
    return (jax.nn.silu(g) * u) @ w_down


def mlstm_chunkwise(q, k, v, ig, lf):
    Bsz, S, H, _ = q.shape
    nc = S // CHUNK

    def to_chunks(t):
        return t.reshape(Bsz, nc, CHUNK, H, -1).transpose(1, 0, 3, 2, 4)

    def gate_chunks(t):
        return t.reshape(Bsz, nc, CHUNK, H).transpose(1, 0, 3, 2)

    causal = jnp.tril(jnp.ones((CHUNK, CHUNK), dtype=bool))

    def body(carry, xs):
        C, n, m = carry
        qc, kc, vc, ic, lfc = xs
        b = jnp.cumsum(lfc, axis=-1)
        b_tot = b[..., -1]
        D = b[..., :, None] - b[..., None, :] + ic[..., None, :]
        D = jnp.where(causal, D, -jnp.inf)
        inter = b + m[..., None]
        m_t = jnp.maximum(inter, jnp.max(D, axis=-1))
        Sw = jnp.einsum('bhtd,bhsd->bhts', qc, kc) * jnp.exp(D - m_t[..., None])
        e_inter = jnp.exp(inter - m_t)
        num = jnp.einsum('bhts,bhsv->bhtv', Sw, vc) + e_inter[..., None] * jnp.einsum('bhvd,bhtd->bhtv', C, qc)
        den = jnp.sum(Sw, axis=-1) + e_inter * jnp.einsum('bhd,bhtd->bht', n, qc)
        h = num / jnp.maximum(jnp.abs(den), jnp.exp(-m_t))[..., None]
        w = b_tot[..., None] - b + ic
        m_new = jnp.maximum(b_tot + m, jnp.max(w, axis=-1))
        decay = jnp.exp(b_tot + m - m_new)
        ws = jnp.exp(w - m_new[..., None])
        C_new = decay[..., None, None] * C + jnp.einsum('bhs,bhsv,bhsd->bhvd', ws, vc, kc)
        n_new = decay[..., None] * n + jnp.einsum('bhs,bhsd->bhd', ws, kc)
        return (C_new, n_new, m_new), h

    init = (jnp.zeros((Bsz, H, v.shape[-1], q.shape[-1]), jnp.float32),
            jnp.zeros((Bsz, H, q.shape[-1]), jnp.float32),
            jnp.zeros((Bsz, H), jnp.float32))
    xs = (to_chunks(q), to_chunks(k), to_chunks(v), gate_chunks(ig), gate_chunks(lf))
    _, hs = lax.scan(body, init, xs)
    return hs.transpose(1, 0, 3, 2, 4).reshape(Bsz, S, H, -1)


def mlstm_mixer(x, w_in, b_gates, head_norm, w_out):
    Bsz, S, _ = x.shape
    proj = x @ w_in
    q, k, v, o, gates = jnp.split(proj, [QK_W, 2 * QK_W, 2 * QK_W + V_W, 2 * QK_W + 2 * V_W], axis=-1)
    gates = gates.astype(jnp.float32) + b_gates.astype(jnp.float32)
    gates = GATE_CAP * jnp.tanh(gates / GATE_CAP)
    ig, fg = jnp.split(gates, 2, axis=-1)
    lf = jax.nn.log_sigmoid(fg)
    qf = q.astype(jnp.float32).reshape(Bsz, S, NH_A, DQK_A) * (DQK_A ** -0.5)
    kf = k.astype(jnp.float32).reshape(Bsz, S, NH_A, DQK_A)
    vf = v.astype(jnp.float32).reshape(Bsz, S, NH_A, DV_A)
    h = mlstm_chunkwise(qf, kf, vf, ig, lf)
    h = h * lax.rsqrt(jnp.mean(h * h, axis=-1, keepdims=True) + EPS)
    h = h * head_norm.astype(jnp.float32).reshape(NH_A, DV_A)
    h = h.reshape(Bsz, S, V_W) * jax.nn.sigmoid(o.astype(jnp.float32))
    return h.astype(x.dtype) @ w_out


def shared_kv(x, kv_norm, w_kv, b_kv):
    Bsz, S, _ = x.shape
    nb = S // BLK
    kv = rmsnorm(x, kv_norm) @ w_kv + b_kv
    k, v = jnp.split(kv, 2, axis=-1)

    def band(t):
        tb = t.reshape(Bsz, nb, BLK, KVH_B, DH_B)
        prev = jnp.concatenate([jnp.zeros_like(tb[:, :1]), tb[:, :-1]], axis=1)
        return jnp.concatenate([prev, tb], axis=2)

    return band(k), band(v)


def swa_sink_mixer(x, k2, v2, w_q, b_q, sinks, w_out, b_out):
    Bsz, S, _ = x.shape
    nb = S // BLK
    q = (x @ w_q + b_q).reshape(Bsz, nb, BLK, KVH_B, GRP_B, DH_B)
    s = jnp.einsum('bnqkgd,bnpkd->bnkgqp', q, k2).astype(jnp.float32) * (DH_B ** -0.5)
    i = jnp.arange(BLK)[:, None]
    j = jnp.arange(2 * BLK)[None, :]
    rel = i + BLK - j
    band = (rel >= 0) & (rel < WINDOW)
    valid = band[None] & ((jnp.arange(nb) > 0)[:, None, None] | (j >= BLK)[None])
    s = jnp.where(valid[None, :, None, None], s, -jnp.inf)
    sk = sinks.astype(jnp.float32).reshape(KVH_B, GRP_B)[None, None, :, :, None, None]
    mx = jnp.maximum(jnp.max(s, axis=-1, keepdims=True), sk)
    p = jnp.exp(s - mx)
    p = p / (jnp.sum(p, axis=-1, keepdims=True) + jnp.exp(sk - mx))
    o = jnp.einsum('bnkgqp,bnpkd->bnqkgd', p.astype(v2.dtype), v2)
    return o.reshape(Bsz, S, NH_B * DH_B) @ w_out + b_out


def setup_inputs(seed: int = 0) -> dict:
    key = jax.random.key(seed)
    ks = jax.random.split(key, 24)
    f32 = jnp.float32
    nrm = lambda k, shape, scale: jax.random.normal(k, shape, f32) * scale
    b_i = nrm(ks[4], (N_A, NH_A), 0.1)
    b_f = 3.0 + nrm(ks[5], (N_A, NH_A), 0.5)
    return {
        "x": nrm(ks[0], (BATCH, SEQ, D_MODEL), 1.0),
        "a_norm": 1.0 + nrm(ks[1], (N_A, D_MODEL), 0.02),
        "a_w_in": nrm(ks[2], (N_A, D_MODEL, IN_A), D_MODEL ** -0.5),
        "a_b_gates": jnp.concatenate([b_i, b_f], axis=-1),
        "a_head_norm": 1.0 + nrm(ks[3], (N_A, V_W), 0.02),
        "a_w_out": nrm(ks[6], (N_A, V_W, D_MODEL), V_W ** -0.5),
        "kv_norm": 1.0 + nrm(ks[7], (D_MODEL,), 0.02),
        "w_kv": nrm(ks[8], (D_MODEL, 2 * KVH_B * DH_B), D_MODEL ** -0.5),
        "b_kv": nrm(ks[9], (2 * KVH_B * DH_B,), 0.02),
        "b_norm": 1.0 + nrm(ks[10], (N_B, D_MODEL), 0.02),
        "b_w_q": nrm(ks[11], (N_B, D_MODEL, NH_B * DH_B), D_MODEL ** -0.5),
        "b_b_q": nrm(ks[12], (N_B, NH_B * DH_B), 0.02),
        "b_sinks": nrm(ks[13], (N_B, NH_B), 0.5),
        "b_w_out": nrm(ks[14], (N_B, NH_B * DH_B, D_MODEL), (NH_B * DH_B) ** -0.5),
        "b_b_out": nrm(ks[15], (N_B, D_MODEL), 0.02),
        "ffn_norm": 1.0 + nrm(ks[16], (DEPTH, D_MODEL), 0.02),
        "w_gate_up": nrm(ks[17], (DEPTH, D_MODEL, 2 * D_FF), D_MODEL ** -0.5),
        "w_down": nrm(ks[18], (DEPTH, D_FF, D_MODEL), D_FF ** -0.5),
        "final_norm": 1.0 + nrm(ks[19], (D_MODEL,), 0.02),
    }


def reference(x, a_norm, a_w_in, a_b_gates, a_head_norm, a_w_out, kv_norm, w_kv, b_kv,
              b_norm, b_w_q, b_b_q, b_sinks, b_w_out, b_b_out, ffn_norm, w_gate_up, w_down, final_norm):
    k2 = v2 = None
    for l in range(DEPTH):
        if l < N_A:
            x = x + mlstm_mixer(rmsnorm(x, a_norm[l]), a_w_in[l], a_b_gates[l], a_head_norm[l], a_w_out[l])
        else:
            j = l - N_A
            x = x + swa_sink_mixer(rmsnorm(x, b_norm[j]), k2, v2, b_w_q[j], b_b_q[j], b_sinks[j], b_w_out[j], b_b_out[j])
        x = x + swiglu(rmsnorm(x, ffn_norm[l]), w_gate_up[l], w_down[l])
        if l == N_A - 1:
            k2, v2 = shared_kv(x, kv_norm, w_kv, b_kv)
    return rmsnorm(x, final_norm)
```

```python
import functools

import jax
import jax.numpy as jnp
from jax import lax
from jax.experimental import pallas as pl
from jax.experimental.pallas import tpu as pltpu

F32 = jnp.float32
BF16 = jnp.bfloat16

D_MODEL = 1024
EPS = 1e-6
NH_A = 4
DV_A = D_MODEL // NH_A
DQK_A = DV_A // 2
GATE_CAP = 15.0
QK_W = NH_A * DQK_A
V_W = NH_A * DV_A
NH_B = 16
KVH_B = 2
GRP_B = NH_B // KVH_B
DH_B = 64
WINDOW = 128
D_FF = 2816

LANES_V7X = 128
MXU_DIM_V7X = 256
VMEM_LIMIT_BYTES_V7X = 58 * 1024 * 1024

MLSTM_CHUNK = 256
FFN_CHUNK = MXU_DIM_V7X
ROW_TILE_A = 256
ROW_TILE_B = 256


def _rms(x, g):
    ms = jnp.mean(x * x, axis=-1, keepdims=True)
    return x * lax.rsqrt(ms + EPS) * g


def _ffn_residual(x1, fnorm, wgu_ref, wd_ref, act_ref):
    xn = _rms(x1, fnorm).astype(BF16)
    for j in range(D_FF // FFN_CHUNK):
        lo = j * FFN_CHUNK
        g = jnp.dot(xn, wgu_ref[:, lo:lo + FFN_CHUNK], preferred_element_type=F32)
        u = jnp.dot(xn, wgu_ref[:, D_FF + lo:D_FF + lo + FFN_CHUNK], preferred_element_type=F32)
        act_ref[:, lo:lo + FFN_CHUNK] = (g * jax.nn.sigmoid(g) * u).astype(BF16)
    return x1 + jnp.dot(act_ref[...], wd_ref[...], preferred_element_type=F32)


def _mlstm_layer_kernel(*refs, tm, emit_kv):
    if emit_kv:
        (x_ref, anorm_ref, wqkvo_ref, wg_ref, bg_ref, hnorm_ref, wout_ref, fnorm_ref, wgu_ref, wd_ref,
         kvnorm_ref, wkv_ref, bkv_ref, o_ref, kv_ref, ct_ref, n_ref, m_ref, hg_ref, act_ref) = refs
    else:
        (x_ref, anorm_ref, wqkvo_ref, wg_ref, bg_ref, hnorm_ref, wout_ref, fnorm_ref, wgu_ref, wd_ref,
         o_ref, ct_ref, n_ref, m_ref, hg_ref, act_ref) = refs
    L = MLSTM_CHUNK

    @pl.when(pl.program_id(0) == 0)
    def _():
        ct_ref[...] = jnp.zeros_like(ct_ref)
        n_ref[...] = jnp.zeros_like(n_ref)
        m_ref[...] = jnp.zeros_like(m_ref)

    x = x_ref[...]
    xn = _rms(x, anorm_ref[...]).astype(BF16)

    gates = jnp.dot(xn, wg_ref[...], preferred_element_type=F32) + bg_ref[...]
    gates = GATE_CAP * jnp.tanh(gates / GATE_CAP)
    lane = lax.broadcasted_iota(jnp.int32, gates.shape, 1)
    log_sig = jnp.minimum(gates, 0.0) - jnp.log1p(jnp.exp(-jnp.abs(gates)))
    gl = jnp.where(lane < NH_A, gates, log_sig)
    t_loc = lax.broadcasted_iota(jnp.int32, gl.shape, 0) & (L - 1)
    b = gl
    k = 1
    while k < L:
        b = b + jnp.where(t_loc >= k, pltpu.roll(b, k, axis=0), 0.0)
        k *= 2
    r_col = jnp.where(lane < NH_A, gl, b)
    r_row = r_col.T

    q_all = jnp.dot(xn, wqkvo_ref[:, 0:QK_W], preferred_element_type=F32)
    q_all = (q_all * (DQK_A ** -0.5)).astype(BF16)
    k_all = jnp.dot(xn, wqkvo_ref[:, QK_W:2 * QK_W], preferred_element_type=F32)
    v_all = jnp.dot(xn, wqkvo_ref[:, 2 * QK_W:2 * QK_W + V_W], preferred_element_type=F32).astype(BF16)
    o_all = jnp.dot(xn, wqkvo_ref[:, 2 * QK_W + V_W:2 * QK_W + 2 * V_W], preferred_element_type=F32)

    row_i = lax.broadcasted_iota(jnp.int32, (L, L), 0)
    col_i = lax.broadcasted_iota(jnp.int32, (L, L), 1)
    causal = col_i <= row_i
    hnorm = hnorm_ref[...]

    for c in range(tm // L):
        lo = c * L
        rc = r_col[lo:lo + L]
        rr = r_row[:, lo:lo + L]
        for h in range(NH_A):
            qh = q_all[lo:lo + L, h * DQK_A:(h + 1) * DQK_A]
            kh = k_all[lo:lo + L, h * DQK_A:(h + 1) * DQK_A]
            vh = v_all[lo:lo + L, h * DV_A:(h + 1) * DV_A]
            i_col = rc[:, h:h + 1]
            b_col = rc[:, NH_A + h:NH_A + h + 1]
            i_row = rr[h:h + 1, :]
            b_row = rr[NH_A + h:NH_A + h + 1, :]
            b_tot = b_row[:, L - 1:L]
            m_prev = m_ref[h:h + 1, 0:1]
            n_prev = n_ref[h:h + 1, :]
            ct = ct_ref[h]

            d = jnp.where(causal, b_col - b_row + i_row, -jnp.inf)
            inter = b_col + m_prev
            m_t = jnp.maximum(inter, jnp.max(d, axis=-1, keepdims=True))
            s = lax.dot_general(qh, kh.astype(BF16), (((1,), (1,)), ((), ())), preferred_element_type=F32)
            sw = s * jnp.exp(d - m_t)
            e_int = jnp.exp(inter - m_t)
            num = (jnp.dot(sw.astype(BF16), vh, preferred_element_type=F32)
                   + e_int * jnp.dot(qh, ct.astype(BF16), preferred_element_type=F32))
            qn = jnp.sum(qh.astype(F32) * n_prev, axis=-1, keepdims=True)
            den = jnp.sum(sw, axis=-1, keepdims=True) + e_int * qn
            hv = num * (1.0 / jnp.maximum(jnp.abs(den), jnp.exp(-m_t)))
            hv = hv * lax.rsqrt(jnp.mean(hv * hv, axis=-1, keepdims=True) + EPS)
            hv = hv * hnorm[:, h * DV_A:(h + 1) * DV_A]
            hv = hv * jax.nn.sigmoid(o_all[lo:lo + L, h * DV_A:(h + 1) * DV_A])
            hg_ref[lo:lo + L, h * DV_A:(h + 1) * DV_A] = hv.astype(BF16)

            w_col = b_tot - b_col + i_col
            w_row = b_tot - b_row + i_row
            m_new = jnp.maximum(b_tot + m_prev, jnp.max(w_row, axis=-1, keepdims=True))
            decay = jnp.exp(b_tot + m_prev - m_new)
            ks = jnp.exp(w_col - m_new) * kh
            ct_ref[h] = decay * ct + lax.dot_general(
                ks.astype(BF16), vh, (((0,), (0,)), ((), ())), preferred_element_type=F32)
            n_ref[h:h + 1, :] = decay * n_prev + jnp.sum(ks, axis=0, keepdims=True)
            m_ref[h:h + 1, :] = jnp.broadcast_to(m_new, (1, LANES_V7X))

    x1 = x + jnp.dot(hg_ref[...], wout_ref[...], preferred_element_type=F32)
    x2 = _ffn_residual(x1, fnorm_ref[...], wgu_ref, wd_ref, act_ref)
    o_ref[...] = x2
    if emit_kv:
        kvn = _rms(x2, kvnorm_ref[...]).astype(BF16)
        kv = jnp.dot(kvn, wkv_ref[...], preferred_element_type=F32) + bkv_ref[...]
        kv_ref[...] = kv.astype(BF16)


def _swa_layer_kernel(*refs, tm, final):
    if final:
        (sink_ref, x_ref, kvc_ref, kvp_ref, bnorm_ref, wq_ref, bq_ref, wo_ref, bo_ref, fnorm_ref, wgu_ref, wd_ref,
         final_ref, o_ref, attn_ref, act_ref) = refs
    else:
        (sink_ref, x_ref, kvc_ref, kvp_ref, bnorm_ref, wq_ref, bq_ref, wo_ref, bo_ref, fnorm_ref, wgu_ref, wd_ref,
         o_ref, attn_ref, act_ref) = refs
    W = WINDOW
    kw = KVH_B * DH_B

    x = x_ref[...]
    xn = _rms(x, bnorm_ref[...]).astype(BF16)
    q = jnp.dot(xn, wq_ref[...], preferred_element_type=F32) + bq_ref[...]
    q = (q * (DH_B ** -0.5)).astype(BF16)

    qi = lax.broadcasted_iota(jnp.int32, (GRP_B * W, 2 * W), 0) & (W - 1)
    kj = lax.broadcasted_iota(jnp.int32, (GRP_B * W, 2 * W), 1)
    band = (kj > qi) & (kj <= qi + W)
    band_first = band & ((kj >= W) | (pl.program_id(0) > 0))

    for b in range(tm // W):
        qb = q[b * W:(b + 1) * W]
        kv_prev = kvp_ref[...] if b == 0 else kvc_ref[(b - 1) * W:b * W, :]
        kvb = jnp.concatenate([kv_prev, kvc_ref[b * W:(b + 1) * W, :]], axis=0)
        valid = band_first if b == 0 else band
        for kh in range(KVH_B):
            kk = kvb[:, kh * DH_B:(kh + 1) * DH_B]
            vv = kvb[:, kw + kh * DH_B:kw + (kh + 1) * DH_B]
            heads = [kh * GRP_B + g for g in range(GRP_B)]
            qs = jnp.concatenate([qb[:, hd * DH_B:(hd + 1) * DH_B] for hd in heads], axis=0)
            sk = jnp.concatenate([jnp.full((W, 1), sink_ref[hd], F32) for hd in heads], axis=0)
            s = lax.dot_general(qs, kk, (((1,), (1,)), ((), ())), preferred_element_type=F32)
            s = jnp.where(valid, s, -jnp.inf)
            mx = jnp.maximum(jnp.max(s, axis=-1, keepdims=True), sk)
            p = jnp.exp(s - mx)
            denom = jnp.sum(p, axis=-1, keepdims=True) + jnp.exp(sk - mx)
            o = jnp.dot(p.astype(BF16), vv, preferred_element_type=F32) * (1.0 / denom)
            for g, hd in enumerate(heads):
                attn_ref[b * W:(b + 1) * W, hd * DH_B:(hd + 1) * DH_B] = o[g * W:(g + 1) * W].astype(BF16)

    x1 = x + jnp.dot(attn_ref[...], wo_ref[...], preferred_element_type=F32) + bo_ref[...]
    x2 = _ffn_residual(x1, fnorm_ref[...], wgu_ref, wd_ref, act_ref)
    if final:
        x2 = _rms(x2, final_ref[...])
    o_ref[...] = x2


def _resident(shape):
    zeros = (0,) * len(shape)
    return pl.BlockSpec(shape, lambda i: zeros, pipeline_mode=pl.Buffered(1))


def _row_tiles(tm, width):
    return pl.BlockSpec((tm, width), lambda i: (i, 0))


def _mlstm_layer(x, anorm, wqkvo, wg, bg, hnorm, wout, fnorm, wgu, wd, kv_params, *, tm):
    S = x.shape[0]
    emit_kv = kv_params is not None
    operands = [x, anorm, wqkvo, wg, bg, hnorm, wout, fnorm, wgu, wd]
    if emit_kv:
        operands += list(kv_params)
    in_specs = [_row_tiles(tm, D_MODEL)] + [_resident(a.shape) for a in operands[1:]]
    out_shape = [jax.ShapeDtypeStruct((S, D_MODEL), F32)]
    out_specs = [_row_tiles(tm, D_MODEL)]
    if emit_kv:
        out_shape.append(jax.ShapeDtypeStruct((S, 2 * KVH_B * DH_B), BF16))
        out_specs.append(_row_tiles(tm, 2 * KVH_B * DH_B))
    out = pl.pallas_call(
        functools.partial(_mlstm_layer_kernel, tm=tm, emit_kv=emit_kv),
        grid=(S // tm,),
        in_specs=in_specs,
        out_specs=out_specs,
        out_shape=out_shape,
        scratch_shapes=[
            pltpu.VMEM((NH_A, DQK_A, DV_A), F32),
            pltpu.VMEM((NH_A, DQK_A), F32),
            pltpu.VMEM((NH_A, LANES_V7X), F32),
            pltpu.VMEM((tm, V_W), BF16),
            pltpu.VMEM((tm, D_FF), BF16),
        ],
        compiler_params=pltpu.CompilerParams(
            dimension_semantics=("arbitrary",), vmem_limit_bytes=VMEM_LIMIT_BYTES_V7X),
        name="mlstm_layer_kv" if emit_kv else "mlstm_layer",
    )(*operands)
    return out if emit_kv else (out[0], None)


def _swa_layer(x, kv, sinks, bnorm, wq, bq, wo, bo, fnorm, wgu, wd, final_norm, *, tm):
    S = x.shape[0]
    final = final_norm is not None
    blocks_per_tile = tm // WINDOW
    operands = [sinks, x, kv, kv, bnorm, wq, bq, wo, bo, fnorm, wgu, wd]
    if final:
        operands.append(final_norm)
    in_specs = [
        pl.BlockSpec(memory_space=pltpu.SMEM),
        _row_tiles(tm, D_MODEL),
        _row_tiles(tm, kv.shape[1]),
        pl.BlockSpec((WINDOW, kv.shape[1]), lambda i: (jnp.maximum(i * blocks_per_tile - 1, 0), 0)),
    ] + [_resident(a.shape) for a in operands[4:]]
    return pl.pallas_call(
        functools.partial(_swa_layer_kernel, tm=tm, final=final),
        grid=(S // tm,),
        in_specs=in_specs,
        out_specs=_row_tiles(tm, D_MODEL),
        out_shape=jax.ShapeDtypeStruct((S, D_MODEL), F32),
        scratch_shapes=[
            pltpu.VMEM((tm, NH_B * DH_B), BF16),
            pltpu.VMEM((tm, D_FF), BF16),
        ],
        compiler_params=pltpu.CompilerParams(
            dimension_semantics=("arbitrary",), vmem_limit_bytes=VMEM_LIMIT_BYTES_V7X),
        name="swa_layer_final" if final else "swa_layer",
    )(*operands)


def kernel(x, a_norm, a_w_in, a_b_gates, a_head_norm, a_w_out, kv_norm, w_kv, b_kv, b_norm, b_w_q, b_b_q,
           b_sinks, b_w_out, b_b_out, ffn_norm, w_gate_up, w_down, final_norm):
    bsz, S, _ = x.shape
    assert bsz == 1 and S % ROW_TILE_A == 0 and S % ROW_TILE_B == 0
    n_a = a_w_in.shape[0]
    n_b = b_w_q.shape[0]
    row = lambda v: v.reshape(1, -1).astype(F32)
    n_qkvo = 2 * QK_W + 2 * V_W
    gate_pad = LANES_V7X - 2 * NH_A

    h = x[0]
    kv = None
    for l in range(n_a):
        wg = jnp.pad(a_w_in[l][:, n_qkvo:], ((0, 0), (0, gate_pad))).astype(BF16)
        bg = jnp.pad(a_b_gates[l], (0, gate_pad)).reshape(1, -1).astype(F32)
        kv_params = None
        if l == n_a - 1:
            kv_params = (row(kv_norm), w_kv.astype(BF16), row(b_kv))
        h, kv = _mlstm_layer(
            h, row(a_norm[l]), a_w_in[l][:, :n_qkvo].astype(BF16), wg, bg, row(a_head_norm[l]),
            a_w_out[l].astype(BF16), row(ffn_norm[l]), w_gate_up[l].astype(BF16), w_down[l].astype(BF16),
            kv_params, tm=ROW_TILE_A)
    for j in range(n_b):
        l = n_a + j
        h = _swa_layer(
            h, kv, b_sinks[j].astype(F32), row(b_norm[j]), b_w_q[j].astype(BF16), row(b_b_q[j]),
            b_w_out[j].astype(BF16), row(b_b_out[j]), row(ffn_norm[l]), w_gate_up[l].astype(BF16),
            w_down[l].astype(BF16), row(final_norm) if j == n_b - 1 else None, tm=ROW_TILE_B)
    return h[None]
```

```python
import functools

import jax
import jax.numpy as jnp
from jax import lax
from jax.experimental import pallas as pl
from jax.experimental.pallas import tpu as pltpu

F32 = jnp.float32
BF16 = jnp.bfloat16

D_MODEL = 1024
EPS = 1e-6
NH_A = 4
DV_A = D_MODEL // NH_A
DQK_A = DV_A // 2
GATE_CAP = 15.0
QK_W = NH_A * DQK_A
V_W = NH_A * DV_A
NH_B = 16
KVH_B = 2
GRP_B = NH_B // KVH_B
DH_B = 64
WINDOW = 128
D_FF = 2816

LANES_V7X = 128
MXU_DIM_V7X = 256
VMEM_LIMIT_BYTES_V7X = 58 * 1024 * 1024

MLSTM_CHUNK = 256
FFN_CHUNK = MXU_DIM_V7X
ROW_TILE_A = 256
ROW_TILE_B = 256


def _rms(x, g):
    ms = jnp.mean(x * x, axis=-1, keepdims=True)
    return x * lax.rsqrt(ms + EPS) * g


def _ffn_residual(x1, fnorm, wgu_ref, wd_ref, act_ref):
    xn = _rms(x1, fnorm).astype(BF16)
    for j in range(D_FF // FFN_CHUNK):
        lo = j * FFN_CHUNK
        g = jnp.dot(xn, wgu_ref[:, lo:lo + FFN_CHUNK], preferred_element_type=F32)
        u = jnp.dot(xn, wgu_ref[:, D_FF + lo:D_FF + lo + FFN_CHUNK], preferred_element_type=F32)
        act_ref[:, lo:lo + FFN_CHUNK] = (g * jax.nn.sigmoid(g) * u).astype(BF16)
    return x1 + jnp.dot(act_ref[...], wd_ref[...], preferred_element_type=F32)


def _mlstm_layer_kernel(*refs, tm, emit_kv):
    if emit_kv:
        (x_ref, anorm_ref, wqkvo_ref, wg_ref, bg_ref, hnorm_ref, wout_ref, fnorm_ref, wgu_ref, wd_ref,
         kvnorm_ref, wkv_ref, bkv_ref, o_ref, kv_ref, ct_ref, n_ref, m_ref, hg_ref, act_ref) = refs
    else:
        (x_ref, anorm_ref, wqkvo_ref, wg_ref, bg_ref, hnorm_ref, wout_ref, fnorm_ref, wgu_ref, wd_ref,
         o_ref, ct_ref, n_ref, m_ref, hg_ref, act_ref) = refs
    L = MLSTM_CHUNK

    @pl.when(pl.program_id(0) == 0)
    def _():
        ct_ref[...] = jnp.zeros_like(ct_ref)
        n_ref[...] = jnp.zeros_like(n_ref)
        m_ref[...] = jnp.zeros_like(m_ref)

    x = x_ref[...]
    xn = _rms(x, anorm_ref[...]).astype(BF16)

    gates = jnp.dot(xn, wg_ref[...], preferred_element_type=F32) + bg_ref[...]
    gates = GATE_CAP * jnp.tanh(gates / GATE_CAP)
    lane = lax.broadcasted_iota(jnp.int32, gates.shape, 1)
    log_sig = jnp.minimum(gates, 0.0) - jnp.log1p(jnp.exp(-jnp.abs(gates)))
    gl = jnp.where(lane < NH_A, gates, log_sig)
    t_loc = lax.broadcasted_iota(jnp.int32, gl.shape, 0) & (L - 1)
    b = gl
    k = 1
    while k < L:
        b = b + jnp.where(t_loc >= k, pltpu.roll(b, k, axis=0), 0.0)
        k *= 2
    r_col = jnp.where(lane < NH_A, gl, b)
    r_row = r_col.T

    q_all = jnp.dot(xn, wqkvo_ref[:, 0:QK_W], preferred_element_type=F32)
    q_all = (q_all * (DQK_A ** -0.5)).astype(BF16)
    k_all = jnp.dot(xn, wqkvo_ref[:, QK_W:2 * QK_W], preferred_element_type=F32)
    v_all = jnp.dot(xn, wqkvo_ref[:, 2 * QK_W:2 * QK_W + V_W], preferred_element_type=F32).astype(BF16)
    o_all = jnp.dot(xn, wqkvo_ref[:, 2 * QK_W + V_W:2 * QK_W + 2 * V_W], preferred_element_type=F32)

    row_i = lax.broadcasted_iota(jnp.int32, (L, L), 0)
    col_i = lax.broadcasted_iota(jnp.int32, (L, L), 1)
    causal = col_i <= row_i
    hnorm = hnorm_ref[...]

    for c in range(tm // L):
        lo = c * L
        rc = r_col[lo:lo + L]
        rr = r_row[:, lo:lo + L]
        for h in range(NH_A):
            qh = q_all[lo:lo + L, h * DQK_A:(h + 1) * DQK_A]
            kh = k_all[lo:lo + L, h * DQK_A:(h + 1) * DQK_A]
            vh = v_all[lo:lo + L, h * DV_A:(h + 1) * DV_A]
            i_col = rc[:, h:h + 1]
            b_col = rc[:, NH_A + h:NH_A + h + 1]
            i_row = rr[h:h + 1, :]
            b_row = rr[NH_A + h:NH_A + h + 1, :]
            b_tot = b_row[:, L - 1:L]
            m_prev = m_ref[h:h + 1, 0:1]
            n_prev = n_ref[h:h + 1, :]
            ct = ct_ref[h]

            d = jnp.where(causal, b_col - b_row + i_row, -jnp.inf)
            inter = b_col + m_prev
            m_t = jnp.maximum(inter, jnp.max(d, axis=-1, keepdims=True))
            s = lax.dot_general(qh, kh.astype(BF16), (((1,), (1,)), ((), ())), preferred_element_type=F32)
            sw = s * jnp.exp(d - m_t)
            e_int = jnp.exp(inter - m_t)
            num = (jnp.dot(sw.astype(BF16), vh, preferred_element_type=F32)
                   + e_int * jnp.dot(qh, ct.astype(BF16), preferred_element_type=F32))
            qn = jnp.sum(qh.astype(F32) * n_prev, axis=-1, keepdims=True)
            den = jnp.sum(sw, axis=-1, keepdims=True) + e_int * qn
            hv = num * (1.0 / jnp.maximum(jnp.abs(den), jnp.exp(-m_t)))
            hv = hv * lax.rsqrt(jnp.mean(hv * hv, axis=-1, keepdims=True) + EPS)
            hv = hv * hnorm[:, h * DV_A:(h + 1) * DV_A]
            hv = hv * jax.nn.sigmoid(o_all[lo:lo + L, h * DV_A:(h + 1) * DV_A])
            hg_ref[lo:lo + L, h * DV_A:(h + 1) * DV_A] = hv.astype(BF16)

            w_col = b_tot - b_col + i_col
            w_row = b_tot - b_row + i_row
            m_new = jnp.maximum(b_tot + m_prev, jnp.max(w_row, axis=-1, keepdims=True))
            decay = jnp.exp(b_tot + m_prev - m_new)
            ks = jnp.exp(w_col - m_new) * kh
            ct_ref[h] = decay * ct + lax.dot_general(
                ks.astype(BF16), vh, (((0,), (0,)), ((), ())), preferred_element_type=F32)
            n_ref[h:h + 1, :] = decay * n_prev + jnp.sum(ks, axis=0, keepdims=True)
            m_ref[h:h + 1, :] = jnp.broadcast_to(m_new, (1, LANES_V7X))

    x1 = x + jnp.dot(hg_ref[...], wout_ref[...], preferred_element_type=F32)
    x2 = _ffn_residual(x1, fnorm_ref[...], wgu_ref, wd_ref, act_ref)
    o_ref[...] = x2
    if emit_kv:
        kvn = _rms(x2, kvnorm_ref[...]).astype(BF16)
        kv = jnp.dot(kvn, wkv_ref[...], preferred_element_type=F32) + bkv_ref[...]
        kv_ref[...] = kv.astype(BF16)


def _swa_layer_kernel(*refs, tm, final):
    if final:
        (sink_ref, x_ref, kvc_ref, kvp_ref, bnorm_ref, wq_ref, bq_ref, wo_ref, bo_ref, fnorm_ref, wgu_ref, wd_ref,
         final_ref, o_ref, attn_ref, act_ref) = refs
    else:
        (sink_ref, x_ref, kvc_ref, kvp_ref, bnorm_ref, wq_ref, bq_ref, wo_ref, bo_ref, fnorm_ref, wgu_ref, wd_ref,
         o_ref, attn_ref, act_ref) = refs
    W = WINDOW
    pairs = GRP_B // 2
    half = pairs * W

    x = x_ref[...]
    xn = _rms(x, bnorm_ref[...]).astype(BF16)
    q = jnp.dot(xn, wq_ref[...], preferred_element_type=F32) + bq_ref[...]
    q = (q * (DH_B ** -0.5)).astype(BF16)

    kj = lax.broadcasted_iota(jnp.int32, (2 * W, GRP_B * W), 0)
    qi = lax.broadcasted_iota(jnp.int32, (2 * W, GRP_B * W), 1) & (W - 1)
    band = (kj > qi) & (kj <= qi + W)
    band_first = band & ((kj >= W) | (pl.program_id(0) > 0))
    low = lax.broadcasted_iota(jnp.int32, (2 * W, LANES_V7X), 1) < DH_B
    zero = jnp.zeros((2 * W, LANES_V7X), BF16)
    nt = (((1,), (1,)), ((), ()))
    tn = (((0,), (0,)), ((), ()))

    for b in range(tm // W):
        qb = q[b * W:(b + 1) * W]
        kv_prev = kvp_ref[...] if b == 0 else kvc_ref[(b - 1) * W:b * W, :]
        kvb = jnp.concatenate([kv_prev, kvc_ref[b * W:(b + 1) * W, :]], axis=0)
        k_01, v_01, k_10, v_10 = (kvb[:, t * LANES_V7X:(t + 1) * LANES_V7X] for t in range(4))
        valid = band_first if b == 0 else band
        for kh in range(KVH_B):
            k_same, k_swap = (k_01, k_10) if kh == 0 else (k_10, k_01)
            v_same, v_swap = (v_01, v_10) if kh == 0 else (v_10, v_01)
            k_even, k_odd = jnp.where(low, k_same, zero), jnp.where(low, zero, k_swap)
            v_even, v_odd = jnp.where(low, v_same, zero), jnp.where(low, zero, v_swap)
            qp = jnp.concatenate(
                [qb[:, (kh * pairs + j) * LANES_V7X:(kh * pairs + j + 1) * LANES_V7X] for j in range(pairs)],
                axis=0)
            s = jnp.concatenate(
                [lax.dot_general(k_even, qp, nt, preferred_element_type=F32),
                 lax.dot_general(k_odd, qp, nt, preferred_element_type=F32)], axis=1)
            heads = [kh * GRP_B + 2 * j for j in range(pairs)] + [kh * GRP_B + 2 * j + 1 for j in range(pairs)]
            sk = jnp.concatenate([jnp.full((1, W), sink_ref[hd], F32) for hd in heads], axis=1)
            s = jnp.where(valid, s, -jnp.inf)
            mx = jnp.maximum(jnp.max(s, axis=0, keepdims=True), sk)
            p = jnp.exp(s - mx)
            inv = 1.0 / (jnp.sum(p, axis=0, keepdims=True) + jnp.exp(sk - mx))
            p = p.astype(BF16)
            o = (lax.dot_general(v_even, p[:, :half], tn, preferred_element_type=F32)
                 + lax.dot_general(v_odd, p[:, half:], tn, preferred_element_type=F32))
            o = o * jnp.concatenate([jnp.broadcast_to(inv[:, :half], (DH_B, half)),
                                     jnp.broadcast_to(inv[:, half:], (DH_B, half))], axis=0)
            for j in range(pairs):
                lo = (kh * pairs + j) * LANES_V7X
                attn_ref[b * W:(b + 1) * W, lo:lo + LANES_V7X] = o[:, j * W:(j + 1) * W].T.astype(BF16)

    x1 = x + jnp.dot(attn_ref[...], wo_ref[...], preferred_element_type=F32) + bo_ref[...]
    x2 = _ffn_residual(x1, fnorm_ref[...], wgu_ref, wd_ref, act_ref)
    if final:
        x2 = _rms(x2, final_ref[...])
    o_ref[...] = x2


def _resident(shape):
    zeros = (0,) * len(shape)
    return pl.BlockSpec(shape, lambda i: zeros, pipeline_mode=pl.Buffered(1))


def _row_tiles(tm, width):
    return pl.BlockSpec((tm, width), lambda i: (i, 0))


def _mlstm_layer(x, anorm, wqkvo, wg, bg, hnorm, wout, fnorm, wgu, wd, kv_params, *, tm):
    S = x.shape[0]
    emit_kv = kv_params is not None
    operands = [x, anorm, wqkvo, wg, bg, hnorm, wout, fnorm, wgu, wd]
    if emit_kv:
        operands += list(kv_params)
    in_specs = [_row_tiles(tm, D_MODEL)] + [_resident(a.shape) for a in operands[1:]]
    out_shape = [jax.ShapeDtypeStruct((S, D_MODEL), F32)]
    out_specs = [_row_tiles(tm, D_MODEL)]
    if emit_kv:
        kv_width = kv_params[1].shape[1]
        out_shape.append(jax.ShapeDtypeStruct((S, kv_width), BF16))
        out_specs.append(_row_tiles(tm, kv_width))
    out = pl.pallas_call(
        functools.partial(_mlstm_layer_kernel, tm=tm, emit_kv=emit_kv),
        grid=(S // tm,),
        in_specs=in_specs,
        out_specs=out_specs,
        out_shape=out_shape,
        scratch_shapes=[
            pltpu.VMEM((NH_A, DQK_A, DV_A), F32),
            pltpu.VMEM((NH_A, DQK_A), F32),
            pltpu.VMEM((NH_A, LANES_V7X), F32),
            pltpu.VMEM((tm, V_W), BF16),
            pltpu.VMEM((tm, D_FF), BF16),
        ],
        compiler_params=pltpu.CompilerParams(
            dimension_semantics=("arbitrary",), vmem_limit_bytes=VMEM_LIMIT_BYTES_V7X),
        name="mlstm_layer_kv" if emit_kv else "mlstm_layer",
    )(*operands)
    return out if emit_kv else (out[0], None)


def _swa_layer(x, kv, sinks, bnorm, wq, bq, wo, bo, fnorm, wgu, wd, final_norm, *, tm):
    S = x.shape[0]
    final = final_norm is not None
    blocks_per_tile = tm // WINDOW
    operands = [sinks, x, kv, kv, bnorm, wq, bq, wo, bo, fnorm, wgu, wd]
    if final:
        operands.append(final_norm)
    in_specs = [
        pl.BlockSpec(memory_space=pltpu.SMEM),
        _row_tiles(tm, D_MODEL),
        _row_tiles(tm, kv.shape[1]),
        pl.BlockSpec((WINDOW, kv.shape[1]), lambda i: (jnp.maximum(i * blocks_per_tile - 1, 0), 0)),
    ] + [_resident(a.shape) for a in operands[4:]]
    return pl.pallas_call(
        functools.partial(_swa_layer_kernel, tm=tm, final=final),
        grid=(S // tm,),
        in_specs=in_specs,
        out_specs=_row_tiles(tm, D_MODEL),
        out_shape=jax.ShapeDtypeStruct((S, D_MODEL), F32),
        scratch_shapes=[
            pltpu.VMEM((tm, NH_B * DH_B), BF16),
            pltpu.VMEM((tm, D_FF), BF16),
        ],
        compiler_params=pltpu.CompilerParams(
            dimension_semantics=("arbitrary",), vmem_limit_bytes=VMEM_LIMIT_BYTES_V7X),
        name="swa_layer_final" if final else "swa_layer",
    )(*operands)


def kernel(x, a_norm, a_w_in, a_b_gates, a_head_norm, a_w_out, kv_norm, w_kv, b_kv, b_norm, b_w_q, b_b_q,
           b_sinks, b_w_out, b_b_out, ffn_norm, w_gate_up, w_down, final_norm):
    bsz, S, _ = x.shape
    assert bsz == 1 and S % ROW_TILE_A == 0 and S % ROW_TILE_B == 0
    n_a = a_w_in.shape[0]
    n_b = b_w_q.shape[0]
    row = lambda v: v.reshape(1, -1).astype(F32)
    n_qkvo = 2 * QK_W + 2 * V_W
    gate_pad = LANES_V7X - 2 * NH_A

    h = x[0]
    kv = None
    for l in range(n_a):
        wg = jnp.pad(a_w_in[l][:, n_qkvo:], ((0, 0), (0, gate_pad))).astype(BF16)
        bg = jnp.pad(a_b_gates[l], (0, gate_pad)).reshape(1, -1).astype(F32)
        kv_params = None
        if l == n_a - 1:
            kw = KVH_B * DH_B
            order = jnp.array([*range(2 * kw), *range(DH_B, kw), *range(DH_B), *range(kw + DH_B, 2 * kw),
                               *range(kw, kw + DH_B)], jnp.int32)
            kv_params = (row(kv_norm), w_kv[:, order].astype(BF16), row(b_kv[order]))
        h, kv = _mlstm_layer(
            h, row(a_norm[l]), a_w_in[l][:, :n_qkvo].astype(BF16), wg, bg, row(a_head_norm[l]),
            a_w_out[l].astype(BF16), row(ffn_norm[l]), w_gate_up[l].astype(BF16), w_down[l].astype(BF16),
            kv_params, tm=ROW_TILE_A)
    for j in range(n_b):
        l = n_a + j
        h = _swa_layer(
            h, kv, b_sinks[j].astype(F32), row(b_norm[j]), b_w_q[j].astype(BF16), row(b_b_q[j]),
            b_w_out[j].astype(BF16), row(b_b_out[j]), row(ffn_norm[l]), w_gate_up[l].astype(BF16),
            w_down[l].astype(BF16), row(final_norm) if j == n_b - 1 else None, tm=ROW_TILE_B)
    return h[None]
```

```python
import functools

import jax
import jax.numpy as jnp
from jax import lax
from jax.experimental import pallas as pl
from jax.experimental.pallas import tpu as pltpu

F32 = jnp.float32
BF16 = jnp.bfloat16

D_MODEL = 1024
EPS = 1e-6
NH_A = 4
DV_A = D_MODEL // NH_A
DQK_A = DV_A // 2
GATE_CAP = 15.0
QK_W = NH_A * DQK_A
V_W = NH_A * DV_A
NH_B = 16
KVH_B = 2
GRP_B = NH_B // KVH_B
DH_B = 64
WINDOW = 128
D_FF = 2816

LANES_V7X = 128
MXU_DIM_V7X = 256
VMEM_LIMIT_BYTES_V7X = 58 * 1024 * 1024

MLSTM_CHUNK = 256
FFN_CHUNK = MXU_DIM_V7X
ROW_TILE_A = 512
ROW_TILE_B = 512


def _rms(x, g):
    ms = jnp.mean(x * x, axis=-1, keepdims=True)
    return x * lax.rsqrt(ms + EPS) * g


def _ffn_steps(x1, fnorm, wgu_ref, wd_ref, act_ref):
    xn = _rms(x1, fnorm).astype(BF16)
    for j in range(D_FF // FFN_CHUNK):
        lo = j * FFN_CHUNK
        gu = jnp.dot(xn, wgu_ref[:, 2 * lo:2 * lo + 2 * FFN_CHUNK], preferred_element_type=F32)
        g, u = gu[:, :FFN_CHUNK], gu[:, FFN_CHUNK:]
        act_ref[:, lo:lo + FFN_CHUNK] = (g * jax.nn.sigmoid(g) * u).astype(BF16)
        yield
    return x1 + jnp.dot(act_ref[...], wd_ref[...], preferred_element_type=F32)


def _interleave(main, side):
    live = [main, side]
    while live:
        for gen in list(live):
            try:
                next(gen)
            except StopIteration:
                live.remove(gen)


def _mlstm_layer_kernel(*refs, tm, emit_kv):
    if emit_kv:
        (x_ref, anorm_ref, wqkvo_ref, wg_ref, bg_ref, hnorm_ref, wout_ref, fnorm_ref, wgu_ref, wd_ref,
         kvnorm_ref, wkv_ref, bkv_ref, o_ref, kv_ref, ct_ref, n_ref, m_ref, hg_ref, hg_prev_ref, x_prev_ref,
         act_ref) = refs
    else:
        (x_ref, anorm_ref, wqkvo_ref, wg_ref, bg_ref, hnorm_ref, wout_ref, fnorm_ref, wgu_ref, wd_ref,
         o_ref, ct_ref, n_ref, m_ref, hg_ref, hg_prev_ref, x_prev_ref, act_ref) = refs
    L = MLSTM_CHUNK

    @pl.when(pl.program_id(0) == 0)
    def _():
        ct_ref[...] = jnp.zeros_like(ct_ref)
        n_ref[...] = jnp.zeros_like(n_ref)
        m_ref[...] = jnp.zeros_like(m_ref)
        hg_prev_ref[...] = jnp.zeros_like(hg_prev_ref)
        x_prev_ref[...] = jnp.zeros_like(x_prev_ref)

    def mixer():
        xn = _rms(x_ref[...], anorm_ref[...]).astype(BF16)

        gates = jnp.dot(xn, wg_ref[...], preferred_element_type=F32) + bg_ref[...]
        gates = GATE_CAP * jnp.tanh(gates / GATE_CAP)
        lane = lax.broadcasted_iota(jnp.int32, gates.shape, 1)
        log_sig = jnp.minimum(gates, 0.0) - jnp.log1p(jnp.exp(-jnp.abs(gates)))
        gl = jnp.where(lane < NH_A, gates, log_sig)
        t_loc = lax.broadcasted_iota(jnp.int32, gl.shape, 0) & (L - 1)
        b = gl
        k = 1
        while k < L:
            b = b + jnp.where(t_loc >= k, pltpu.roll(b, k, axis=0), 0.0)
            k *= 2
        r_col = jnp.where(lane < NH_A, gl, b)
        r_row = r_col.T

        q_all = jnp.dot(xn, wqkvo_ref[:, 0:QK_W], preferred_element_type=F32)
        q_all = (q_all * (DQK_A ** -0.5)).astype(BF16)
        k_all = jnp.dot(xn, wqkvo_ref[:, QK_W:2 * QK_W], preferred_element_type=F32)
        v_all = jnp.dot(xn, wqkvo_ref[:, 2 * QK_W:2 * QK_W + V_W], preferred_element_type=F32).astype(BF16)
        o_all = jnp.dot(xn, wqkvo_ref[:, 2 * QK_W + V_W:2 * QK_W + 2 * V_W], preferred_element_type=F32)
        yield

        row_i = lax.broadcasted_iota(jnp.int32, (L, L), 0)
        col_i = lax.broadcasted_iota(jnp.int32, (L, L), 1)
        causal = col_i <= row_i
        hnorm = hnorm_ref[...]

        for c in range(tm // L):
            lo = c * L
            rc = r_col[lo:lo + L]
            rr = r_row[:, lo:lo + L]
            for h in range(NH_A):
                qh = q_all[lo:lo + L, h * DQK_A:(h + 1) * DQK_A]
                kh = k_all[lo:lo + L, h * DQK_A:(h + 1) * DQK_A]
                vh = v_all[lo:lo + L, h * DV_A:(h + 1) * DV_A]
                i_col = rc[:, h:h + 1]
                b_col = rc[:, NH_A + h:NH_A + h + 1]
                i_row = rr[h:h + 1, :]
                b_row = rr[NH_A + h:NH_A + h + 1, :]
                b_tot = b_row[:, L - 1:L]
                m_prev = m_ref[h:h + 1, 0:1]

                s = lax.dot_general(qh, kh.astype(BF16), (((1,), (1,)), ((), ())), preferred_element_type=F32)
                d = jnp.where(causal, b_col - b_row + i_row, -jnp.inf)
                inter = b_col + m_prev
                m_t = jnp.maximum(inter, jnp.max(d, axis=-1, keepdims=True))
                sw = s * jnp.exp(d - m_t)
                e_int = jnp.exp(inter - m_t)
                sw_sum = jnp.sum(sw, axis=-1, keepdims=True)
                sw = sw.astype(BF16)
                w_col = b_tot - b_col + i_col
                w_row = b_tot - b_row + i_row
                m_new = jnp.maximum(b_tot + m_prev, jnp.max(w_row, axis=-1, keepdims=True))
                decay = jnp.exp(b_tot + m_prev - m_new)
                ks = jnp.exp(w_col - m_new) * kh
                ks_sum = jnp.sum(ks, axis=0, keepdims=True)
                ks = ks.astype(BF16)
                yield

                n_prev = n_ref[h:h + 1, :]
                ct = ct_ref[h]
                num = (jnp.dot(sw, vh, preferred_element_type=F32)
                       + e_int * jnp.dot(qh, ct.astype(BF16), preferred_element_type=F32))
                ct_ref[h] = decay * ct + lax.dot_general(
                    ks, vh, (((0,), (0,)), ((), ())), preferred_element_type=F32)
                n_ref[h:h + 1, :] = decay * n_prev + ks_sum
                m_ref[h:h + 1, :] = jnp.broadcast_to(m_new, (1, LANES_V7X))
                qn = jnp.sum(qh.astype(F32) * n_prev, axis=-1, keepdims=True)
                den = sw_sum + e_int * qn
                hv = num * (1.0 / jnp.maximum(jnp.abs(den), jnp.exp(-m_t)))
                hv = hv * lax.rsqrt(jnp.mean(hv * hv, axis=-1, keepdims=True) + EPS)
                hv = hv * hnorm[:, h * DV_A:(h + 1) * DV_A]
                hv = hv * jax.nn.sigmoid(o_all[lo:lo + L, h * DV_A:(h + 1) * DV_A])
                hg_ref[lo:lo + L, h * DV_A:(h + 1) * DV_A] = hv.astype(BF16)

    def post():
        x1 = x_prev_ref[...] + jnp.dot(hg_prev_ref[...], wout_ref[...], preferred_element_type=F32)
        yield
        x2 = yield from _ffn_steps(x1, fnorm_ref[...], wgu_ref, wd_ref, act_ref)
        o_ref[...] = x2
        if emit_kv:
            kvn = _rms(x2, kvnorm_ref[...]).astype(BF16)
            kv = jnp.dot(kvn, wkv_ref[...], preferred_element_type=F32) + bkv_ref[...]
            kv_ref[...] = kv.astype(BF16)

    _interleave(mixer(), post())
    hg_prev_ref[...] = hg_ref[...]
    x_prev_ref[...] = x_ref[...]


def _swa_layer_kernel(*refs, tm, final):
    if final:
        (sink_ref, x_ref, kvc_ref, kvp_ref, bnorm_ref, wq_ref, bq_ref, wo_ref, bo_ref, fnorm_ref, wgu_ref, wd_ref,
         final_ref, o_ref, attn_ref, attn_prev_ref, x_prev_ref, act_ref) = refs
    else:
        (sink_ref, x_ref, kvc_ref, kvp_ref, bnorm_ref, wq_ref, bq_ref, wo_ref, bo_ref, fnorm_ref, wgu_ref, wd_ref,
         o_ref, attn_ref, attn_prev_ref, x_prev_ref, act_ref) = refs

    @pl.when(pl.program_id(0) == 0)
    def _():
        attn_prev_ref[...] = jnp.zeros_like(attn_prev_ref)
        x_prev_ref[...] = jnp.zeros_like(x_prev_ref)

    W = WINDOW
    pairs = GRP_B // 2
    half = pairs * W

    kj = lax.broadcasted_iota(jnp.int32, (2 * W, GRP_B * W), 0)
    qi = lax.broadcasted_iota(jnp.int32, (2 * W, GRP_B * W), 1) & (W - 1)
    band = (kj > qi) & (kj <= qi + W)
    band_first = band & ((kj >= W) | (pl.program_id(0) > 0))
    low = lax.broadcasted_iota(jnp.int32, (2 * W, LANES_V7X), 1) < DH_B
    zero = jnp.zeros((2 * W, LANES_V7X), BF16)
    nt = (((1,), (1,)), ((), ()))
    tn = (((0,), (0,)), ((), ()))

    def mixer():
        xn = _rms(x_ref[...], bnorm_ref[...]).astype(BF16)
        q = jnp.dot(xn, wq_ref[...], preferred_element_type=F32) + bq_ref[...]
        q = (q * (DH_B ** -0.5)).astype(BF16)
        for b in range(tm // W):
            k0 = b * W
            qb = q[b * W:(b + 1) * W]
            kv_prev = kvp_ref[...] if k0 == 0 else kvc_ref[k0 - W:k0, :]
            kvb = jnp.concatenate([kv_prev, kvc_ref[k0:k0 + W, :]], axis=0)
            k_01, v_01, k_10, v_10 = (kvb[:, c * LANES_V7X:(c + 1) * LANES_V7X] for c in range(4))
            valid = band_first if k0 == 0 else band
            for kh in range(KVH_B):
                k_same, k_swap = (k_01, k_10) if kh == 0 else (k_10, k_01)
                v_same, v_swap = (v_01, v_10) if kh == 0 else (v_10, v_01)
                k_even, k_odd = jnp.where(low, k_same, zero), jnp.where(low, zero, k_swap)
                v_even, v_odd = jnp.where(low, v_same, zero), jnp.where(low, zero, v_swap)
                qp = jnp.concatenate(
                    [qb[:, (kh * pairs + j) * LANES_V7X:(kh * pairs + j + 1) * LANES_V7X] for j in range(pairs)],
                    axis=0)
                s = jnp.concatenate(
                    [lax.dot_general(k_even, qp, nt, preferred_element_type=F32),
                     lax.dot_general(k_odd, qp, nt, preferred_element_type=F32)], axis=1)
                heads = ([kh * GRP_B + 2 * j for j in range(pairs)]
                         + [kh * GRP_B + 2 * j + 1 for j in range(pairs)])
                sk = jnp.concatenate([jnp.full((1, W), sink_ref[hd], F32) for hd in heads], axis=1)
                s = jnp.where(valid, s, -jnp.inf)
                mx = jnp.maximum(jnp.max(s, axis=0, keepdims=True), sk)
                p = jnp.exp(s - mx)
                inv = 1.0 / (jnp.sum(p, axis=0, keepdims=True) + jnp.exp(sk - mx))
                p = p.astype(BF16)
                yield
                o = (lax.dot_general(v_even, p[:, :half], tn, preferred_element_type=F32)
                     + lax.dot_general(v_odd, p[:, half:], tn, preferred_element_type=F32))
                o = o * jnp.concatenate([jnp.broadcast_to(inv[:, :half], (DH_B, half)),
                                         jnp.broadcast_to(inv[:, half:], (DH_B, half))], axis=0)
                for j in range(pairs):
                    lo = (kh * pairs + j) * LANES_V7X
                    attn_ref[k0:k0 + W, lo:lo + LANES_V7X] = o[:, j * W:(j + 1) * W].T.astype(BF16)

    def post():
        x1 = (x_prev_ref[...] + jnp.dot(attn_prev_ref[...], wo_ref[...], preferred_element_type=F32)
              + bo_ref[...])
        yield
        x2 = yield from _ffn_steps(x1, fnorm_ref[...], wgu_ref, wd_ref, act_ref)
        if final:
            x2 = _rms(x2, final_ref[...])
        o_ref[...] = x2

    _interleave(mixer(), post())
    attn_prev_ref[...] = attn_ref[...]
    x_prev_ref[...] = x_ref[...]


def _resident(shape):
    zeros = (0,) * len(shape)
    return pl.BlockSpec(shape, lambda i: zeros, pipeline_mode=pl.Buffered(1))


def _mlstm_layer(x, anorm, wqkvo, wg, bg, hnorm, wout, fnorm, wgu, wd, kv_params, *, tm):
    S = x.shape[0]
    emit_kv = kv_params is not None
    n_tiles = S // tm
    operands = [x, anorm, wqkvo, wg, bg, hnorm, wout, fnorm, wgu, wd]
    if emit_kv:
        operands += list(kv_params)
    in_specs = ([pl.BlockSpec((tm, D_MODEL), lambda i: (jnp.minimum(i, n_tiles - 1), 0))]
                + [_resident(a.shape) for a in operands[1:]])
    prev_tile = lambda i: (jnp.maximum(i - 1, 0), 0)
    out_shape = [jax.ShapeDtypeStruct((S, D_MODEL), F32)]
    out_specs = [pl.BlockSpec((tm, D_MODEL), prev_tile)]
    if emit_kv:
        kv_width = kv_params[1].shape[1]
        out_shape.append(jax.ShapeDtypeStruct((S, kv_width), BF16))
        out_specs.append(pl.BlockSpec((tm, kv_width), prev_tile))
    out = pl.pallas_call(
        functools.partial(_mlstm_layer_kernel, tm=tm, emit_kv=emit_kv),
        grid=(n_tiles + 1,),
        in_specs=in_specs,
        out_specs=out_specs,
        out_shape=out_shape,
        scratch_shapes=[
            pltpu.VMEM((NH_A, DQK_A, DV_A), F32),
            pltpu.VMEM((NH_A, DQK_A), F32),
            pltpu.VMEM((NH_A, LANES_V7X), F32),
            pltpu.VMEM((tm, V_W), BF16),
            pltpu.VMEM((tm, V_W), BF16),
            pltpu.VMEM((tm, D_MODEL), F32),
            pltpu.VMEM((tm, D_FF), BF16),
        ],
        compiler_params=pltpu.CompilerParams(
            dimension_semantics=("arbitrary",), vmem_limit_bytes=VMEM_LIMIT_BYTES_V7X),
        name="mlstm_layer_kv" if emit_kv else "mlstm_layer",
    )(*operands)
    return out if emit_kv else (out[0], None)


def _swa_layer(x, kv, sinks, bnorm, wq, bq, wo, bo, fnorm, wgu, wd, final_norm, *, tm):
    S = x.shape[0]
    final = final_norm is not None
    n_tiles = S // tm
    blocks_per_tile = tm // WINDOW
    operands = [sinks, x, kv, kv, bnorm, wq, bq, wo, bo, fnorm, wgu, wd]
    if final:
        operands.append(final_norm)
    mixer_tile = lambda i: jnp.minimum(i, n_tiles - 1)
    in_specs = [
        pl.BlockSpec(memory_space=pltpu.SMEM),
        pl.BlockSpec((tm, D_MODEL), lambda i: (mixer_tile(i), 0)),
        pl.BlockSpec((tm, kv.shape[1]), lambda i: (mixer_tile(i), 0)),
        pl.BlockSpec((WINDOW, kv.shape[1]), lambda i: (jnp.maximum(mixer_tile(i) * blocks_per_tile - 1, 0), 0)),
    ] + [_resident(a.shape) for a in operands[4:]]
    return pl.pallas_call(
        functools.partial(_swa_layer_kernel, tm=tm, final=final),
        grid=(n_tiles + 1,),
        in_specs=in_specs,
        out_specs=pl.BlockSpec((tm, D_MODEL), lambda i: (jnp.maximum(i - 1, 0), 0)),
        out_shape=jax.ShapeDtypeStruct((S, D_MODEL), F32),
        scratch_shapes=[
            pltpu.VMEM((tm, NH_B * DH_B), BF16),
            pltpu.VMEM((tm, NH_B * DH_B), BF16),
            pltpu.VMEM((tm, D_MODEL), F32),
            pltpu.VMEM((tm, D_FF), BF16),
        ],
        compiler_params=pltpu.CompilerParams(
            dimension_semantics=("arbitrary",), vmem_limit_bytes=VMEM_LIMIT_BYTES_V7X),
        name="swa_layer_final" if final else "swa_layer",
    )(*operands)


def kernel(x, a_norm, a_w_in, a_b_gates, a_head_norm, a_w_out, kv_norm, w_kv, b_kv, b_norm, b_w_q, b_b_q,
           b_sinks, b_w_out, b_b_out, ffn_norm, w_gate_up, w_down, final_norm):
    bsz, S, _ = x.shape
    assert bsz == 1 and S % ROW_TILE_A == 0 and S % ROW_TILE_B == 0
    n_a = a_w_in.shape[0]
    n_b = b_w_q.shape[0]
    row = lambda v: v.reshape(1, -1).astype(F32)

    def chunked_gate_up(w):
        w = w.reshape(D_MODEL, 2, D_FF // FFN_CHUNK, FFN_CHUNK).transpose(0, 2, 1, 3)
        return w.reshape(D_MODEL, 2 * D_FF).astype(BF16)

    n_qkvo = 2 * QK_W + 2 * V_W
    gate_pad = LANES_V7X - 2 * NH_A

    h = x[0]
    kv = None
    for l in range(n_a):
        wg = jnp.pad(a_w_in[l][:, n_qkvo:], ((0, 0), (0, gate_pad))).astype(BF16)
        bg = jnp.pad(a_b_gates[l], (0, gate_pad)).reshape(1, -1).astype(F32)
        kv_params = None
        if l == n_a - 1:
            kw = KVH_B * DH_B
            order = jnp.array([*range(2 * kw), *range(DH_B, kw), *range(DH_B), *range(kw + DH_B, 2 * kw),
                               *range(kw, kw + DH_B)], jnp.int32)
            kv_params = (row(kv_norm), w_kv[:, order].astype(BF16), row(b_kv[order]))
        h, kv = _mlstm_layer(
            h, row(a_norm[l]), a_w_in[l][:, :n_qkvo].astype(BF16), wg, bg, row(a_head_norm[l]),
            a_w_out[l].astype(BF16), row(ffn_norm[l]), chunked_gate_up(w_gate_up[l]), w_down[l].astype(BF16),
            kv_params, tm=ROW_TILE_A)
    for j in range(n_b):
        l = n_a + j
        h = _swa_layer(
            h, kv, b_sinks[j].astype(F32), row(b_norm[j]), b_w_q[j].astype(BF16), row(b_b_q[j]),
            b_w_out[j].astype(BF16), row(b_b_out[j]), row(ffn_norm[l]), chunked_gate_up(w_gate_up[l]),
            w_down[l].astype(BF16), row(final_norm) if j == n_b - 1 else None, tm=ROW_TILE_B)
    return h[None]
```

```python
import functools

import jax
import jax.numpy as jnp
from jax import lax
from jax.experimental import pallas as pl
from jax.experimental.pallas import tpu as pltpu

F32 = jnp.float32
BF16 = jnp.bfloat16

D_MODEL = 1024
EPS = 1e-6
NH_A = 4
DV_A = D_MODEL // NH_A
DQK_A = DV_A // 2
GATE_CAP = 15.0
QK_W = NH_A * DQK_A
V_W = NH_A * DV_A
NH_B = 16
KVH_B = 2
GRP_B = NH_B // KVH_B
DH_B = 64
WINDOW = 128
D_FF = 2816

LANES_V7X = 128
MXU_DIM_V7X = 256
VMEM_LIMIT_BYTES_V7X = 58 * 1024 * 1024

MLSTM_CHUNK = 256
FFN_CHUNK = MXU_DIM_V7X
ROW_TILE_A = 512
ROW_TILE_B = 512


def _rms(x, g):
    ms = jnp.mean(x * x, axis=-1, keepdims=True)
    return x * lax.rsqrt(ms + EPS) * g


def _ffn_steps(x1, fnorm, wgu_ref, wd_ref, act_ref):
    xn = _rms(x1, fnorm).astype(BF16)
    for j in range(D_FF // FFN_CHUNK):
        lo = j * FFN_CHUNK
        g = jnp.dot(xn, wgu_ref[:, lo:lo + FFN_CHUNK], preferred_element_type=F32)
        u = jnp.dot(xn, wgu_ref[:, D_FF + lo:D_FF + lo + FFN_CHUNK], preferred_element_type=F32)
        act_ref[:, lo:lo + FFN_CHUNK] = (g * jax.nn.sigmoid(g) * u).astype(BF16)
        yield
    return x1 + jnp.dot(act_ref[...], wd_ref[...], preferred_element_type=F32)


def _interleave(main, side):
    live = [main, side]
    while live:
        for gen in list(live):
            try:
                next(gen)
            except StopIteration:
                live.remove(gen)


def _mlstm_layer_kernel(*refs, tm, emit_kv):
    if emit_kv:
        (x_ref, anorm_ref, wqkvo_ref, wg_ref, bg_ref, hnorm_ref, wout_ref, fnorm_ref, wgu_ref, wd_ref,
         kvnorm_ref, wkv_ref, bkv_ref, o_ref, kv_ref, ct_ref, n_ref, m_ref, hg_ref, hg_prev_ref, x_prev_ref,
         act_ref) = refs
    else:
        (x_ref, anorm_ref, wqkvo_ref, wg_ref, bg_ref, hnorm_ref, wout_ref, fnorm_ref, wgu_ref, wd_ref,
         o_ref, ct_ref, n_ref, m_ref, hg_ref, hg_prev_ref, x_prev_ref, act_ref) = refs
    L = MLSTM_CHUNK

    @pl.when(pl.program_id(0) == 0)
    def _():
        ct_ref[...] = jnp.zeros_like(ct_ref)
        n_ref[...] = jnp.zeros_like(n_ref)
        m_ref[...] = jnp.zeros_like(m_ref)
        hg_prev_ref[...] = jnp.zeros_like(hg_prev_ref)
        x_prev_ref[...] = jnp.zeros_like(x_prev_ref)

    def mixer():
        xn = _rms(x_ref[...], anorm_ref[...]).astype(BF16)

        gates = jnp.dot(xn, wg_ref[...], preferred_element_type=F32) + bg_ref[...]
        gates = GATE_CAP * jnp.tanh(gates / GATE_CAP)
        lane = lax.broadcasted_iota(jnp.int32, gates.shape, 1)
        log_sig = jnp.minimum(gates, 0.0) - jnp.log1p(jnp.exp(-jnp.abs(gates)))
        gl = jnp.where(lane < NH_A, gates, log_sig)
        t_loc = lax.broadcasted_iota(jnp.int32, gl.shape, 0) & (L - 1)
        b = gl
        k = 1
        while k < L:
            b = b + jnp.where(t_loc >= k, pltpu.roll(b, k, axis=0), 0.0)
            k *= 2
        r_col = jnp.where(lane < NH_A, gl, b)
        r_row = r_col.T

        q_all = jnp.dot(xn, wqkvo_ref[:, 0:QK_W], preferred_element_type=F32)
        q_all = (q_all * (DQK_A ** -0.5)).astype(BF16)
        k_all = jnp.dot(xn, wqkvo_ref[:, QK_W:2 * QK_W], preferred_element_type=F32)
        v_all = jnp.dot(xn, wqkvo_ref[:, 2 * QK_W:2 * QK_W + V_W], preferred_element_type=F32).astype(BF16)
        o_all = jnp.dot(xn, wqkvo_ref[:, 2 * QK_W + V_W:2 * QK_W + 2 * V_W], preferred_element_type=F32)
        yield

        row_i = lax.broadcasted_iota(jnp.int32, (L, L), 0)
        col_i = lax.broadcasted_iota(jnp.int32, (L, L), 1)
        causal = col_i <= row_i
        hnorm = hnorm_ref[...]

        for c in range(tm // L):
            lo = c * L
            rc = r_col[lo:lo + L]
            rr = r_row[:, lo:lo + L]
            for h in range(NH_A):
                qh = q_all[lo:lo + L, h * DQK_A:(h + 1) * DQK_A]
                kh = k_all[lo:lo + L, h * DQK_A:(h + 1) * DQK_A]
                vh = v_all[lo:lo + L, h * DV_A:(h + 1) * DV_A]
                i_col = rc[:, h:h + 1]
                b_col = rc[:, NH_A + h:NH_A + h + 1]
                i_row = rr[h:h + 1, :]
                b_row = rr[NH_A + h:NH_A + h + 1, :]
                b_tot = b_row[:, L - 1:L]
                m_prev = m_ref[h:h + 1, 0:1]

                s = lax.dot_general(qh, kh.astype(BF16), (((1,), (1,)), ((), ())), preferred_element_type=F32)
                d = jnp.where(causal, b_col - b_row + i_row, -jnp.inf)
                inter = b_col + m_prev
                m_t = jnp.maximum(inter, jnp.max(d, axis=-1, keepdims=True))
                sw = s * jnp.exp(d - m_t)
                e_int = jnp.exp(inter - m_t)
                sw_sum = jnp.sum(sw, axis=-1, keepdims=True)
                sw = sw.astype(BF16)
                w_col = b_tot - b_col + i_col
                w_row = b_tot - b_row + i_row
                m_new = jnp.maximum(b_tot + m_prev, jnp.max(w_row, axis=-1, keepdims=True))
                decay = jnp.exp(b_tot + m_prev - m_new)
                ks = jnp.exp(w_col - m_new) * kh
                ks_sum = jnp.sum(ks, axis=0, keepdims=True)
                ks = ks.astype(BF16)
                yield

                n_prev = n_ref[h:h + 1, :]
                ct = ct_ref[h]
                num = (jnp.dot(sw, vh, preferred_element_type=F32)
                       + e_int * jnp.dot(qh, ct.astype(BF16), preferred_element_type=F32))
                ct_ref[h] = decay * ct + lax.dot_general(
                    ks, vh, (((0,), (0,)), ((), ())), preferred_element_type=F32)
                n_ref[h:h + 1, :] = decay * n_prev + ks_sum
                m_ref[h:h + 1, :] = jnp.broadcast_to(m_new, (1, LANES_V7X))
                qn = jnp.sum(qh.astype(F32) * n_prev, axis=-1, keepdims=True)
                den = sw_sum + e_int * qn
                hv = num * (1.0 / jnp.maximum(jnp.abs(den), jnp.exp(-m_t)))
                hv = hv * lax.rsqrt(jnp.mean(hv * hv, axis=-1, keepdims=True) + EPS)
                hv = hv * hnorm[:, h * DV_A:(h + 1) * DV_A]
                hv = hv * jax.nn.sigmoid(o_all[lo:lo + L, h * DV_A:(h + 1) * DV_A])
                hg_ref[lo:lo + L, h * DV_A:(h + 1) * DV_A] = hv.astype(BF16)

    def post():
        x1 = x_prev_ref[...] + jnp.dot(hg_prev_ref[...], wout_ref[...], preferred_element_type=F32)
        yield
        x2 = yield from _ffn_steps(x1, fnorm_ref[...], wgu_ref, wd_ref, act_ref)
        o_ref[...] = x2
        if emit_kv:
            kvn = _rms(x2, kvnorm_ref[...]).astype(BF16)
            kv = jnp.dot(kvn, wkv_ref[...], preferred_element_type=F32) + bkv_ref[...]
            kv_ref[...] = kv.astype(BF16)

    _interleave(mixer(), post())
    hg_prev_ref[...] = hg_ref[...]
    x_prev_ref[...] = x_ref[...]


def _swa_layer_kernel(*refs, tm, final):
    if final:
        (sink_ref, x_ref, kvc_ref, kvp_ref, bnorm_ref, wq_ref, bq_ref, wo_ref, bo_ref, fnorm_ref, wgu_ref, wd_ref,
         final_ref, o_ref, attn_ref, attn_prev_ref, x_prev_ref, act_ref) = refs
    else:
        (sink_ref, x_ref, kvc_ref, kvp_ref, bnorm_ref, wq_ref, bq_ref, wo_ref, bo_ref, fnorm_ref, wgu_ref, wd_ref,
         o_ref, attn_ref, attn_prev_ref, x_prev_ref, act_ref) = refs

    @pl.when(pl.program_id(0) == 0)
    def _():
        attn_prev_ref[...] = jnp.zeros_like(attn_prev_ref)
        x_prev_ref[...] = jnp.zeros_like(x_prev_ref)

    W = WINDOW
    pairs = GRP_B // 2
    half = pairs * W

    kj = lax.broadcasted_iota(jnp.int32, (2 * W, GRP_B * W), 0)
    qi = lax.broadcasted_iota(jnp.int32, (2 * W, GRP_B * W), 1) & (W - 1)
    band = (kj > qi) & (kj <= qi + W)
    band_first = band & ((kj >= W) | (pl.program_id(0) > 0))
    low = lax.broadcasted_iota(jnp.int32, (2 * W, LANES_V7X), 1) < DH_B
    zero = jnp.zeros((2 * W, LANES_V7X), BF16)
    nt = (((1,), (1,)), ((), ()))
    tn = (((0,), (0,)), ((), ()))

    def mixer():
        xn = _rms(x_ref[...], bnorm_ref[...]).astype(BF16)
        q = jnp.dot(xn, wq_ref[...], preferred_element_type=F32) + bq_ref[...]
        q = (q * (DH_B ** -0.5)).astype(BF16)
        for b in range(tm // W):
            k0 = b * W
            qb = q[b * W:(b + 1) * W]
            kv_prev = kvp_ref[...] if k0 == 0 else kvc_ref[k0 - W:k0, :]
            kvb = jnp.concatenate([kv_prev, kvc_ref[k0:k0 + W, :]], axis=0)
            k_01, v_01, k_10, v_10 = (kvb[:, c * LANES_V7X:(c + 1) * LANES_V7X] for c in range(4))
            valid = band_first if k0 == 0 else band
            for kh in range(KVH_B):
                k_same, k_swap = (k_01, k_10) if kh == 0 else (k_10, k_01)
                v_same, v_swap = (v_01, v_10) if kh == 0 else (v_10, v_01)
                k_even, k_odd = jnp.where(low, k_same, zero), jnp.where(low, zero, k_swap)
                v_even, v_odd = jnp.where(low, v_same, zero), jnp.where(low, zero, v_swap)
                qp = jnp.concatenate(
                    [qb[:, (kh * pairs + j) * LANES_V7X:(kh * pairs + j + 1) * LANES_V7X] for j in range(pairs)],
                    axis=0)
                s = jnp.concatenate(
                    [lax.dot_general(k_even, qp, nt, preferred_element_type=F32),
                     lax.dot_general(k_odd, qp, nt, preferred_element_type=F32)], axis=1)
                heads = ([kh * GRP_B + 2 * j for j in range(pairs)]
                         + [kh * GRP_B + 2 * j + 1 for j in range(pairs)])
                sk = jnp.concatenate([jnp.full((1, W), sink_ref[hd], F32) for hd in heads], axis=1)
                s = jnp.where(valid, s, -jnp.inf)
                mx = jnp.maximum(jnp.max(s, axis=0, keepdims=True), sk)
                p = jnp.exp(s - mx)
                inv = 1.0 / (jnp.sum(p, axis=0, keepdims=True) + jnp.exp(sk - mx))
                p = p.astype(BF16)
                yield
                o = (lax.dot_general(v_even, p[:, :half], tn, preferred_element_type=F32)
                     + lax.dot_general(v_odd, p[:, half:], tn, preferred_element_type=F32))
                o = o * jnp.concatenate([jnp.broadcast_to(inv[:, :half], (DH_B, half)),
                                         jnp.broadcast_to(inv[:, half:], (DH_B, half))], axis=0)
                for j in range(pairs):
                    lo = (kh * pairs + j) * LANES_V7X
                    attn_ref[k0:k0 + W, lo:lo + LANES_V7X] = o[:, j * W:(j + 1) * W].T.astype(BF16)

    def post():
        x1 = (x_prev_ref[...] + jnp.dot(attn_prev_ref[...], wo_ref[...], preferred_element_type=F32)
              + bo_ref[...])
        yield
        x2 = yield from _ffn_steps(x1, fnorm_ref[...], wgu_ref, wd_ref, act_ref)
        if final:
            x2 = _rms(x2, final_ref[...])
        o_ref[...] = x2

    _interleave(mixer(), post())
    attn_prev_ref[...] = attn_ref[...]
    x_prev_ref[...] = x_ref[...]


def _resident(operand):
    if isinstance(operand, tuple):
        stack, layer = operand
        cols = stack.shape[2] // LANES_V7X * LANES_V7X
        return pl.BlockSpec((None, stack.shape[1], cols), lambda i: (layer, 0, 0), pipeline_mode=pl.Buffered(1))
    return pl.BlockSpec(operand.shape, lambda i: (0,) * operand.ndim, pipeline_mode=pl.Buffered(1))


def _mlstm_layer(x, anorm, wqkvo, wg, bg, hnorm, wout, fnorm, wgu, wd, kv_params, *, tm):
    S = x.shape[0]
    emit_kv = kv_params is not None
    n_tiles = S // tm
    operands = [x, anorm, wqkvo, wg, bg, hnorm, wout, fnorm, wgu, wd]
    if emit_kv:
        operands += list(kv_params)
    in_specs = ([pl.BlockSpec((tm, D_MODEL), lambda i: (jnp.minimum(i, n_tiles - 1), 0))]
                + [_resident(a) for a in operands[1:]])
    prev_tile = lambda i: (jnp.maximum(i - 1, 0), 0)
    out_shape = [jax.ShapeDtypeStruct((S, D_MODEL), F32)]
    out_specs = [pl.BlockSpec((tm, D_MODEL), prev_tile)]
    if emit_kv:
        kv_width = kv_params[1].shape[1]
        out_shape.append(jax.ShapeDtypeStruct((S, kv_width), BF16))
        out_specs.append(pl.BlockSpec((tm, kv_width), prev_tile))
    out = pl.pallas_call(
        functools.partial(_mlstm_layer_kernel, tm=tm, emit_kv=emit_kv),
        grid=(n_tiles + 1,),
        in_specs=in_specs,
        out_specs=out_specs,
        out_shape=out_shape,
        scratch_shapes=[
            pltpu.VMEM((NH_A, DQK_A, DV_A), F32),
            pltpu.VMEM((NH_A, DQK_A), F32),
            pltpu.VMEM((NH_A, LANES_V7X), F32),
            pltpu.VMEM((tm, V_W), BF16),
            pltpu.VMEM((tm, V_W), BF16),
            pltpu.VMEM((tm, D_MODEL), F32),
            pltpu.VMEM((tm, D_FF), BF16),
        ],
        compiler_params=pltpu.CompilerParams(
            dimension_semantics=("arbitrary",), vmem_limit_bytes=VMEM_LIMIT_BYTES_V7X),
        name="mlstm_layer_kv" if emit_kv else "mlstm_layer",
    )(*[a[0] if isinstance(a, tuple) else a for a in operands])
    return out if emit_kv else (out[0], None)


def _swa_layer(x, kv, sinks, bnorm, wq, bq, wo, bo, fnorm, wgu, wd, final_norm, *, tm):
    S = x.shape[0]
    final = final_norm is not None
    n_tiles = S // tm
    blocks_per_tile = tm // WINDOW
    operands = [sinks, x, kv, kv, bnorm, wq, bq, wo, bo, fnorm, wgu, wd]
    if final:
        operands.append(final_norm)
    mixer_tile = lambda i: jnp.minimum(i, n_tiles - 1)
    in_specs = [
        pl.BlockSpec(memory_space=pltpu.SMEM),
        pl.BlockSpec((tm, D_MODEL), lambda i: (mixer_tile(i), 0)),
        pl.BlockSpec((tm, kv.shape[1]), lambda i: (mixer_tile(i), 0)),
        pl.BlockSpec((WINDOW, kv.shape[1]), lambda i: (jnp.maximum(mixer_tile(i) * blocks_per_tile - 1, 0), 0)),
    ] + [_resident(a) for a in operands[4:]]
    return pl.pallas_call(
        functools.partial(_swa_layer_kernel, tm=tm, final=final),
        grid=(n_tiles + 1,),
        in_specs=in_specs,
        out_specs=pl.BlockSpec((tm, D_MODEL), lambda i: (jnp.maximum(i - 1, 0), 0)),
        out_shape=jax.ShapeDtypeStruct((S, D_MODEL), F32),
        scratch_shapes=[
            pltpu.VMEM((tm, NH_B * DH_B), BF16),
            pltpu.VMEM((tm, NH_B * DH_B), BF16),
            pltpu.VMEM((tm, D_MODEL), F32),
            pltpu.VMEM((tm, D_FF), BF16),
        ],
        compiler_params=pltpu.CompilerParams(
            dimension_semantics=("arbitrary",), vmem_limit_bytes=VMEM_LIMIT_BYTES_V7X),
        name="swa_layer_final" if final else "swa_layer",
    )(*[a[0] if isinstance(a, tuple) else a for a in operands])


def kernel(x, a_norm, a_w_in, a_b_gates, a_head_norm, a_w_out, kv_norm, w_kv, b_kv, b_norm, b_w_q, b_b_q,
           b_sinks, b_w_out, b_b_out, ffn_norm, w_gate_up, w_down, final_norm):
    bsz, S, _ = x.shape
    assert bsz == 1 and S % ROW_TILE_A == 0 and S % ROW_TILE_B == 0
    n_a = a_w_in.shape[0]
    n_b = b_w_q.shape[0]
    row = lambda v: v.reshape(1, -1).astype(F32)
    n_qkvo = 2 * QK_W + 2 * V_W
    gate_pad = LANES_V7X - 2 * NH_A
    w_in, w_out_a = a_w_in.astype(BF16), a_w_out.astype(BF16)
    w_q, w_out_b = b_w_q.astype(BF16), b_w_out.astype(BF16)
    w_gu, w_d = w_gate_up.astype(BF16), w_down.astype(BF16)

    h = x[0]
    kv = None
    for l in range(n_a):
        wg = jnp.pad(a_w_in[l][:, n_qkvo:], ((0, 0), (0, gate_pad))).astype(BF16)
        bg = jnp.pad(a_b_gates[l], (0, gate_pad)).reshape(1, -1).astype(F32)
        kv_params = None
        if l == n_a - 1:
            kw = KVH_B * DH_B
            order = jnp.array([*range(2 * kw), *range(DH_B, kw), *range(DH_B), *range(kw + DH_B, 2 * kw),
                               *range(kw, kw + DH_B)], jnp.int32)
            kv_params = (row(kv_norm), w_kv[:, order].astype(BF16), row(b_kv[order]))
        h, kv = _mlstm_layer(
            h, row(a_norm[l]), (w_in, l), wg, bg, row(a_head_norm[l]), (w_out_a, l), row(ffn_norm[l]),
            (w_gu, l), (w_d, l), kv_params, tm=ROW_TILE_A)
    for j in range(n_b):
        l = n_a + j
        h = _swa_layer(
            h, kv, b_sinks[j].astype(F32), row(b_norm[j]), (w_q, j), row(b_b_q[j]), (w_out_b, j),
            row(b_b_out[j]), row(ffn_norm[l]), (w_gu, l), (w_d, l),
            row(final_norm) if j == n_b - 1 else None, tm=ROW_TILE_B)
    return h[None]
```

```python
import functools

import jax
import jax.numpy as jnp
from jax import lax
from jax.experimental import pallas as pl
from jax.experimental.pallas import tpu as pltpu

F32 = jnp.float32
BF16 = jnp.bfloat16

D_MODEL = 1024
EPS = 1e-6
NH_A = 4
DV_A = D_MODEL // NH_A
DQK_A = DV_A // 2
GATE_CAP = 15.0
LOG2_E = 1.4426950408889634
QK_W = NH_A * DQK_A
V_W = NH_A * DV_A
NH_B = 16
KVH_B = 2
GRP_B = NH_B // KVH_B
DH_B = 64
WINDOW = 128
D_FF = 2816

LANES_V7X = 128
MXU_DIM_V7X = 256
VMEM_LIMIT_BYTES_V7X = 58 * 1024 * 1024

MLSTM_CHUNK = 256
FFN_CHUNK = MXU_DIM_V7X
ROW_TILE_A = 512
ROW_TILE_B = 512


def _rms(x, g):
    ms = jnp.mean(x * x, axis=-1, keepdims=True)
    return x * lax.rsqrt(ms + EPS) * g


def _ffn_steps(x1, fnorm, wgu_ref, wd_ref, act_ref):
    xn = _rms(x1, fnorm).astype(BF16)
    for j in range(D_FF // FFN_CHUNK):
        lo = j * FFN_CHUNK
        g = jnp.dot(xn, wgu_ref[:, lo:lo + FFN_CHUNK], preferred_element_type=F32)
        u = jnp.dot(xn, wgu_ref[:, D_FF + lo:D_FF + lo + FFN_CHUNK], preferred_element_type=F32)
        act_ref[:, lo:lo + FFN_CHUNK] = (g * jax.nn.sigmoid(g) * u).astype(BF16)
        yield
    return x1 + jnp.dot(act_ref[...], wd_ref[...], preferred_element_type=F32)


def _interleave(main, side):
    live = [main, side]
    while live:
        for gen in list(live):
            try:
                next(gen)
            except StopIteration:
                live.remove(gen)


def _run(gen):
    for _ in gen:
        pass


def _pipelined_steps(mixer, post, keep_for_next_step):
    i = pl.program_id(0)
    last = pl.num_programs(0) - 1

    @pl.when(i == 0)
    def _():
        _run(mixer())

    @pl.when(jnp.logical_and(i > 0, i < last))
    def _():
        _interleave(mixer(), post())

    @pl.when(i == last)
    def _():
        _run(post())

    @pl.when(i < last)
    def _():
        keep_for_next_step()


def _mlstm_layer_kernel(*refs, tm, emit_kv):
    if emit_kv:
        (x_ref, anorm_ref, wqkvo_ref, wg_ref, bg_ref, hnorm_ref, wout_ref, fnorm_ref, wgu_ref, wd_ref,
         kvnorm_ref, wkv_ref, bkv_ref, o_ref, kv_ref, ct_ref, n_ref, m_ref, hg_ref, hg_prev_ref, x_prev_ref,
         act_ref) = refs
    else:
        (x_ref, anorm_ref, wqkvo_ref, wg_ref, bg_ref, hnorm_ref, wout_ref, fnorm_ref, wgu_ref, wd_ref,
         o_ref, ct_ref, n_ref, m_ref, hg_ref, hg_prev_ref, x_prev_ref, act_ref) = refs
    L = MLSTM_CHUNK

    @pl.when(pl.program_id(0) == 0)
    def _():
        ct_ref[...] = jnp.zeros_like(ct_ref)
        n_ref[...] = jnp.zeros_like(n_ref)
        m_ref[...] = jnp.zeros_like(m_ref)

    def mixer():
        xn = _rms(x_ref[...], anorm_ref[...]).astype(BF16)

        gates = jnp.dot(xn, wg_ref[...], preferred_element_type=F32) + bg_ref[...]
        gates = GATE_CAP * jnp.tanh(gates / GATE_CAP)
        lane = lax.broadcasted_iota(jnp.int32, gates.shape, 1)
        log_sig = jnp.minimum(gates, 0.0) - jnp.log1p(jnp.exp(-jnp.abs(gates)))
        gl = jnp.where(lane < NH_A, gates, log_sig)
        t_loc = lax.broadcasted_iota(jnp.int32, gl.shape, 0) & (L - 1)
        b = gl
        k = 1
        while k < L:
            b = b + jnp.where(t_loc >= k, pltpu.roll(b, k, axis=0), 0.0)
            k *= 2
        r_col = jnp.where(lane < NH_A, gl, b)
        r_row = r_col.T

        q_all = jnp.dot(xn, wqkvo_ref[:, 0:QK_W], preferred_element_type=F32)
        q_all = (q_all * (DQK_A ** -0.5)).astype(BF16)
        k_all = jnp.dot(xn, wqkvo_ref[:, QK_W:2 * QK_W], preferred_element_type=F32)
        v_all = jnp.dot(xn, wqkvo_ref[:, 2 * QK_W:2 * QK_W + V_W], preferred_element_type=F32).astype(BF16)
        o_all = jnp.dot(xn, wqkvo_ref[:, 2 * QK_W + V_W:2 * QK_W + 2 * V_W], preferred_element_type=F32)
        yield

        row_i = lax.broadcasted_iota(jnp.int32, (L, L), 0)
        col_i = lax.broadcasted_iota(jnp.int32, (L, L), 1)
        causal = col_i <= row_i
        hnorm = hnorm_ref[...]

        for c in range(tm // L):
            lo = c * L
            rc = r_col[lo:lo + L]
            rr = r_row[:, lo:lo + L]
            for h in range(NH_A):
                qh = q_all[lo:lo + L, h * DQK_A:(h + 1) * DQK_A]
                kh = k_all[lo:lo + L, h * DQK_A:(h + 1) * DQK_A]
                vh = v_all[lo:lo + L, h * DV_A:(h + 1) * DV_A]
                i_col = rc[:, h:h + 1]
                b_col = rc[:, NH_A + h:NH_A + h + 1]
                i_row = rr[h:h + 1, :]
                b_row = rr[NH_A + h:NH_A + h + 1, :]
                b_tot = b_row[:, L - 1:L]
                m_prev = m_ref[h:h + 1, 0:1]

                s = lax.dot_general(qh, kh.astype(BF16), (((1,), (1,)), ((), ())), preferred_element_type=F32)
                d = jnp.where(causal, b_col - b_row + i_row, -jnp.inf)
                inter = b_col + m_prev
                m_t = jnp.maximum(inter, jnp.max(d, axis=-1, keepdims=True))
                sw = s * jnp.exp(d - m_t)
                e_int = jnp.exp(inter - m_t)
                sw_sum = jnp.sum(sw, axis=-1, keepdims=True)
                sw = sw.astype(BF16)
                w_col = b_tot - b_col + i_col
                w_row = b_tot - b_row + i_row
                m_new = jnp.maximum(b_tot + m_prev, jnp.max(w_row, axis=-1, keepdims=True))
                decay = jnp.exp(b_tot + m_prev - m_new)
                ks = jnp.exp(w_col - m_new) * kh
                ks_sum = jnp.sum(ks, axis=0, keepdims=True)
                ks = ks.astype(BF16)
                yield

                n_prev = n_ref[h:h + 1, :]
                ct = ct_ref[h]
                num = (jnp.dot(sw, vh, preferred_element_type=F32)
                       + e_int * jnp.dot(qh, ct.astype(BF16), preferred_element_type=F32))
                ct_ref[h] = decay * ct + lax.dot_general(
                    ks, vh, (((0,), (0,)), ((), ())), preferred_element_type=F32)
                n_ref[h:h + 1, :] = decay * n_prev + ks_sum
                m_ref[h:h + 1, :] = jnp.broadcast_to(m_new, (1, LANES_V7X))
                qn = jnp.sum(qh.astype(F32) * n_prev, axis=-1, keepdims=True)
                den = sw_sum + e_int * qn
                hv = num * (1.0 / jnp.maximum(jnp.abs(den), jnp.exp(-m_t)))
                hv = hv * lax.rsqrt(jnp.mean(hv * hv, axis=-1, keepdims=True) + EPS)
                hv = hv * hnorm[:, h * DV_A:(h + 1) * DV_A]
                hv = hv * jax.nn.sigmoid(o_all[lo:lo + L, h * DV_A:(h + 1) * DV_A])
                hg_ref[lo:lo + L, h * DV_A:(h + 1) * DV_A] = hv.astype(BF16)

    def post():
        x1 = x_prev_ref[...] + jnp.dot(hg_prev_ref[...], wout_ref[...], preferred_element_type=F32)
        yield
        x2 = yield from _ffn_steps(x1, fnorm_ref[...], wgu_ref, wd_ref, act_ref)
        o_ref[...] = x2
        if emit_kv:
            kvn = _rms(x2, kvnorm_ref[...]).astype(BF16)
            kv = jnp.dot(kvn, wkv_ref[...], preferred_element_type=F32) + bkv_ref[...]
            kv_ref[...] = kv.astype(BF16)

    def keep_for_next_step():
        hg_prev_ref[...] = hg_ref[...]
        x_prev_ref[...] = x_ref[...]

    _pipelined_steps(mixer, post, keep_for_next_step)


def _swa_layer_kernel(*refs, tm, final):
    if final:
        (sink_ref, x_ref, kvc_ref, kvp_ref, bnorm_ref, wq_ref, bq_ref, wo_ref, bo_ref, fnorm_ref, wgu_ref, wd_ref,
         final_ref, o_ref, attn_ref, attn_prev_ref, x_prev_ref, act_ref) = refs
    else:
        (sink_ref, x_ref, kvc_ref, kvp_ref, bnorm_ref, wq_ref, bq_ref, wo_ref, bo_ref, fnorm_ref, wgu_ref, wd_ref,
         o_ref, attn_ref, attn_prev_ref, x_prev_ref, act_ref) = refs

    W = WINDOW
    pairs = GRP_B // 2
    half = pairs * W

    kj = lax.broadcasted_iota(jnp.int32, (2 * W, GRP_B * W), 0)
    qi = lax.broadcasted_iota(jnp.int32, (2 * W, GRP_B * W), 1) & (W - 1)
    band = (kj > qi) & (kj <= qi + W)
    band_first = band & ((kj >= W) | (pl.program_id(0) > 0))
    lane_i = lax.broadcasted_iota(jnp.int32, (2 * W, LANES_V7X), 1)
    low = lane_i < DH_B
    zero = jnp.zeros((2 * W, LANES_V7X), BF16)
    ones_lane_hi = jnp.where(lane_i == DH_B, 1.0, 0.0).astype(BF16)
    ones_lane_lo = jnp.where(lane_i == 0, 1.0, 0.0).astype(BF16)
    nt = (((1,), (1,)), ((), ()))
    tn = (((0,), (0,)), ((), ()))

    def mixer():
        xn = _rms(x_ref[...], bnorm_ref[...]).astype(BF16)
        q = jnp.dot(xn, wq_ref[...], preferred_element_type=F32) + bq_ref[...]
        q = (q * (DH_B ** -0.5 * LOG2_E)).astype(BF16)
        for b in range(tm // W):
            k0 = b * W
            qb = q[b * W:(b + 1) * W]
            kv_prev = kvp_ref[...] if k0 == 0 else kvc_ref[k0 - W:k0, :]
            kvb = jnp.concatenate([kv_prev, kvc_ref[k0:k0 + W, :]], axis=0)
            k_01, v_01, k_10, v_10 = (kvb[:, c * LANES_V7X:(c + 1) * LANES_V7X] for c in range(4))
            valid = band_first if k0 == 0 else band
            for kh in range(KVH_B):
                k_same, k_swap = (k_01, k_10) if kh == 0 else (k_10, k_01)
                v_same, v_swap = (v_01, v_10) if kh == 0 else (v_10, v_01)
                k_even, k_odd = jnp.where(low, k_same, zero), jnp.where(low, zero, k_swap)
                v_even = jnp.where(low, v_same, ones_lane_hi)
                v_odd = jnp.where(low, ones_lane_lo, v_swap)
                qp = jnp.concatenate(
                    [qb[:, (kh * pairs + j) * LANES_V7X:(kh * pairs + j + 1) * LANES_V7X] for j in range(pairs)],
                    axis=0)
                s = jnp.concatenate(
                    [lax.dot_general(k_even, qp, nt, preferred_element_type=F32),
                     lax.dot_general(k_odd, qp, nt, preferred_element_type=F32)], axis=1)
                heads = ([kh * GRP_B + 2 * j for j in range(pairs)]
                         + [kh * GRP_B + 2 * j + 1 for j in range(pairs)])
                sk = jnp.concatenate([jnp.full((1, W), sink_ref[hd] * LOG2_E, F32) for hd in heads], axis=1)
                s = jnp.where(valid, s, -jnp.inf)
                mx = jnp.maximum(jnp.max(s, axis=0, keepdims=True), sk)
                p = jnp.exp2(s - mx).astype(BF16)
                sink_p = jnp.exp2(sk - mx)
                yield
                o_even = lax.dot_general(v_even, p[:, :half], tn, preferred_element_type=F32)
                o_odd = lax.dot_general(v_odd, p[:, half:], tn, preferred_element_type=F32)
                inv_even = 1.0 / (o_even[DH_B:DH_B + 1, :] + sink_p[:, :half])
                inv_odd = 1.0 / (o_odd[0:1, :] + sink_p[:, half:])
                o = jnp.concatenate([o_even[:DH_B] * inv_even, o_odd[DH_B:] * inv_odd], axis=0)
                for j in range(pairs):
                    lo = (kh * pairs + j) * LANES_V7X
                    attn_ref[k0:k0 + W, lo:lo + LANES_V7X] = o[:, j * W:(j + 1) * W].T.astype(BF16)

    def post():
        x1 = (x_prev_ref[...] + jnp.dot(attn_prev_ref[...], wo_ref[...], preferred_element_type=F32)
              + bo_ref[...])
        yield
        x2 = yield from _ffn_steps(x1, fnorm_ref[...], wgu_ref, wd_ref, act_ref)
        if final:
            x2 = _rms(x2, final_ref[...])
        o_ref[...] = x2

    def keep_for_next_step():
        attn_prev_ref[...] = attn_ref[...]
        x_prev_ref[...] = x_ref[...]

    _pipelined_steps(mixer, post, keep_for_next_step)


def _resident(operand):
    if isinstance(operand, tuple):
        stack, layer = operand
        cols = stack.shape[2] // LANES_V7X * LANES_V7X
        return pl.BlockSpec((None, stack.shape[1], cols), lambda i: (layer, 0, 0), pipeline_mode=pl.Buffered(1))
    return pl.BlockSpec(operand.shape, lambda i: (0,) * operand.ndim, pipeline_mode=pl.Buffered(1))


def _mlstm_layer(x, anorm, wqkvo, wg, bg, hnorm, wout, fnorm, wgu, wd, kv_params, *, tm):
    S = x.shape[0]
    emit_kv = kv_params is not None
    n_tiles = S // tm
    operands = [x, anorm, wqkvo, wg, bg, hnorm, wout, fnorm, wgu, wd]
    if emit_kv:
        operands += list(kv_params)
    in_specs = ([pl.BlockSpec((tm, D_MODEL), lambda i: (jnp.minimum(i, n_tiles - 1), 0))]
                + [_resident(a) for a in operands[1:]])
    prev_tile = lambda i: (jnp.maximum(i - 1, 0), 0)
    out_shape = [jax.ShapeDtypeStruct((S, D_MODEL), F32)]
    out_specs = [pl.BlockSpec((tm, D_MODEL), prev_tile)]
    if emit_kv:
        kv_width = kv_params[1].shape[1]
        out_shape.append(jax.ShapeDtypeStruct((S, kv_width), BF16))
        out_specs.append(pl.BlockSpec((tm, kv_width), prev_tile))
    out = pl.pallas_call(
        functools.partial(_mlstm_layer_kernel, tm=tm, emit_kv=emit_kv),
        grid=(n_tiles + 1,),
        in_specs=in_specs,
        out_specs=out_specs,
        out_shape=out_shape,
        scratch_shapes=[
            pltpu.VMEM((NH_A, DQK_A, DV_A), F32),
            pltpu.VMEM((NH_A, DQK_A), F32),
            pltpu.VMEM((NH_A, LANES_V7X), F32),
            pltpu.VMEM((tm, V_W), BF16),
            pltpu.VMEM((tm, V_W), BF16),
            pltpu.VMEM((tm, D_MODEL), F32),
            pltpu.VMEM((tm, D_FF), BF16),
        ],
        compiler_params=pltpu.CompilerParams(
            dimension_semantics=("arbitrary",), vmem_limit_bytes=VMEM_LIMIT_BYTES_V7X),
        name="mlstm_layer_kv" if emit_kv else "mlstm_layer",
    )(*[a[0] if isinstance(a, tuple) else a for a in operands])
    return out if emit_kv else (out[0], None)


def _swa_layer(x, kv, sinks, bnorm, wq, bq, wo, bo, fnorm, wgu, wd, final_norm, *, tm):
    S = x.shape[0]
    final = final_norm is not None
    n_tiles = S // tm
    blocks_per_tile = tm // WINDOW
    operands = [sinks, x, kv, kv, bnorm, wq, bq, wo, bo, fnorm, wgu, wd]
    if final:
        operands.append(final_norm)
    mixer_tile = lambda i: jnp.minimum(i, n_tiles - 1)
    in_specs = [
        pl.BlockSpec(memory_space=pltpu.SMEM),
        pl.BlockSpec((tm, D_MODEL), lambda i: (mixer_tile(i), 0)),
        pl.BlockSpec((tm, kv.shape[1]), lambda i: (mixer_tile(i), 0)),
        pl.BlockSpec((WINDOW, kv.shape[1]), lambda i: (jnp.maximum(mixer_tile(i) * blocks_per_tile - 1, 0), 0)),
    ] + [_resident(a) for a in operands[4:]]
    return pl.pallas_call(
        functools.partial(_swa_layer_kernel, tm=tm, final=final),
        grid=(n_tiles + 1,),
        in_specs=in_specs,
        out_specs=pl.BlockSpec((tm, D_MODEL), lambda i: (jnp.maximum(i - 1, 0), 0)),
        out_shape=jax.ShapeDtypeStruct((S, D_MODEL), F32),
        scratch_shapes=[
            pltpu.VMEM((tm, NH_B * DH_B), BF16),
            pltpu.VMEM((tm, NH_B * DH_B), BF16),
            pltpu.VMEM((tm, D_MODEL), F32),
            pltpu.VMEM((tm, D_FF), BF16),
        ],
        compiler_params=pltpu.CompilerParams(
            dimension_semantics=("arbitrary",), vmem_limit_bytes=VMEM_LIMIT_BYTES_V7X),
        name="swa_layer_final" if final else "swa_layer",
    )(*[a[0] if isinstance(a, tuple) else a for a in operands])


def kernel(x, a_norm, a_w_in, a_b_gates, a_head_norm, a_w_out, kv_norm, w_kv, b_kv, b_norm, b_w_q, b_b_q,
           b_sinks, b_w_out, b_b_out, ffn_norm, w_gate_up, w_down, final_norm):
    bsz, S, _ = x.shape
    assert bsz == 1 and S % ROW_TILE_A == 0 and S % ROW_TILE_B == 0
    n_a = a_w_in.shape[0]
    n_b = b_w_q.shape[0]
    row = lambda v: v.reshape(1, -1).astype(F32)
    n_qkvo = 2 * QK_W + 2 * V_W
    gate_pad = LANES_V7X - 2 * NH_A
    w_in, w_out_a = a_w_in.astype(BF16), a_w_out.astype(BF16)
    w_q, w_out_b = b_w_q.astype(BF16), b_w_out.astype(BF16)
    w_gu, w_d = w_gate_up.astype(BF16), w_down.astype(BF16)

    h = x[0]
    kv = None
    for l in range(n_a):
        wg = jnp.pad(a_w_in[l][:, n_qkvo:], ((0, 0), (0, gate_pad))).astype(BF16)
        bg = jnp.pad(a_b_gates[l], (0, gate_pad)).reshape(1, -1).astype(F32)
        kv_params = None
        if l == n_a - 1:
            kw = KVH_B * DH_B
            order = jnp.array([*range(2 * kw), *range(DH_B, kw), *range(DH_B), *range(kw + DH_B, 2 * kw),
                               *range(kw, kw + DH_B)], jnp.int32)
            kv_params = (row(kv_norm), w_kv[:, order].astype(BF16), row(b_kv[order]))
        h, kv = _mlstm_layer(
            h, row(a_norm[l]), (w_in, l), wg, bg, row(a_head_norm[l]), (w_out_a, l), row(ffn_norm[l]),
            (w_gu, l), (w_d, l), kv_params, tm=ROW_TILE_A)
    for j in range(n_b):
        l = n_a + j
        h = _swa_layer(
            h, kv, b_sinks[j].astype(F32), row(b_norm[j]), (w_q, j), row(b_b_q[j]), (w_out_b, j),
            row(b_b_out[j]), row(ffn_norm[l]), (w_gu, l), (w_d, l),
            row(final_norm) if j == n_b - 1 else None, tm=ROW_TILE_B)
    return h[None]
```

```python
import functools

import jax
import jax.numpy as jnp
from jax import lax
from jax.experimental import pallas as pl
from jax.experimental.pallas import tpu as pltpu

F32 = jnp.float32
BF16 = jnp.bfloat16

D_MODEL = 1024
EPS = 1e-6
NH_A = 4
DV_A = D_MODEL // NH_A
DQK_A = DV_A // 2
GATE_CAP = 15.0
LOG2_E = 1.4426950408889634
QK_W = NH_A * DQK_A
V_W = NH_A * DV_A
NH_B = 16
KVH_B = 2
GRP_B = NH_B // KVH_B
DH_B = 64
WINDOW = 128
D_FF = 2816

LANES_V7X = 128
MXU_DIM_V7X = 256
VMEM_LIMIT_BYTES_V7X = 58 * 1024 * 1024

MLSTM_CHUNK = 256
FFN_CHUNK = MXU_DIM_V7X
ROW_TILE_A = 512
ROW_TILE_B = 512


def _rms(x, g):
    ms = jnp.mean(x * x, axis=-1, keepdims=True)
    return x * lax.rsqrt(ms + EPS) * g


def _ffn_steps(x1, fnorm, wgu_ref, wd_ref, act_ref, fine=False):
    xn = _rms(x1, fnorm).astype(BF16)
    for j in range(D_FF // FFN_CHUNK):
        lo = j * FFN_CHUNK
        g = jnp.dot(xn, wgu_ref[:, lo:lo + FFN_CHUNK], preferred_element_type=F32)
        if fine:
            yield
        u = jnp.dot(xn, wgu_ref[:, D_FF + lo:D_FF + lo + FFN_CHUNK], preferred_element_type=F32)
        act_ref[:, lo:lo + FFN_CHUNK] = (g * jax.nn.sigmoid(g) * u).astype(BF16)
        yield
    return x1 + jnp.dot(act_ref[...], wd_ref[...], preferred_element_type=F32)


def _interleave(main, side):
    live = [main, side]
    while live:
        for gen in list(live):
            try:
                next(gen)
            except StopIteration:
                live.remove(gen)


def _run(gen):
    for _ in gen:
        pass


def _pipelined_steps(mixer, post, keep_for_next_step):
    i = pl.program_id(0)
    last = pl.num_programs(0) - 1

    @pl.when(i == 0)
    def _():
        _run(mixer())

    @pl.when(jnp.logical_and(i > 0, i < last))
    def _():
        _interleave(mixer(), post())

    @pl.when(i == last)
    def _():
        _run(post())

    @pl.when(i < last)
    def _():
        keep_for_next_step()


def _mlstm_layer_kernel(*refs, tm, emit_kv):
    if emit_kv:
        (x_ref, anorm_ref, wqkvo_ref, wg_ref, bg_ref, hnorm_ref, wout_ref, fnorm_ref, wgu_ref, wd_ref,
         kvnorm_ref, wkv_ref, bkv_ref, o_ref, kv_ref, ct_ref, n_ref, m_ref, hg_ref, hg_prev_ref, x_prev_ref,
         act_ref) = refs
    else:
        (x_ref, anorm_ref, wqkvo_ref, wg_ref, bg_ref, hnorm_ref, wout_ref, fnorm_ref, wgu_ref, wd_ref,
         o_ref, ct_ref, n_ref, m_ref, hg_ref, hg_prev_ref, x_prev_ref, act_ref) = refs
    L = MLSTM_CHUNK

    @pl.when(pl.program_id(0) == 0)
    def _():
        ct_ref[...] = jnp.zeros_like(ct_ref)
        n_ref[...] = jnp.zeros_like(n_ref)
        m_ref[...] = jnp.zeros_like(m_ref)

    def mixer():
        xn = _rms(x_ref[...], anorm_ref[...]).astype(BF16)

        gates = jnp.dot(xn, wg_ref[...], preferred_element_type=F32) + bg_ref[...]
        gates = GATE_CAP * jnp.tanh(gates / GATE_CAP)
        lane = lax.broadcasted_iota(jnp.int32, gates.shape, 1)
        log_sig = jnp.minimum(gates, 0.0) - jnp.log1p(jnp.exp(-jnp.abs(gates)))
        gl = jnp.where(lane < NH_A, gates, log_sig)
        t_loc = lax.broadcasted_iota(jnp.int32, gl.shape, 0) & (L - 1)
        b = gl
        k = 1
        while k < L:
            b = b + jnp.where(t_loc >= k, pltpu.roll(b, k, axis=0), 0.0)
            k *= 2
        r_col = jnp.where(lane < NH_A, gl, b)
        r_row = r_col.T

        q_all = jnp.dot(xn, wqkvo_ref[:, 0:QK_W], preferred_element_type=F32)
        q_all = (q_all * (DQK_A ** -0.5)).astype(BF16)
        k_all = jnp.dot(xn, wqkvo_ref[:, QK_W:2 * QK_W], preferred_element_type=F32)
        v_all = jnp.dot(xn, wqkvo_ref[:, 2 * QK_W:2 * QK_W + V_W], preferred_element_type=F32).astype(BF16)
        o_all = jnp.dot(xn, wqkvo_ref[:, 2 * QK_W + V_W:2 * QK_W + 2 * V_W], preferred_element_type=F32)
        yield

        row_i = lax.broadcasted_iota(jnp.int32, (L, L), 0)
        col_i = lax.broadcasted_iota(jnp.int32, (L, L), 1)
        causal = col_i <= row_i
        hnorm = hnorm_ref[...]

        def weights(c, h):
            lo = c * L
            rc = r_col[lo:lo + L]
            rr = r_row[:, lo:lo + L]
            qh = q_all[lo:lo + L, h * DQK_A:(h + 1) * DQK_A]
            kh = k_all[lo:lo + L, h * DQK_A:(h + 1) * DQK_A]
            i_col = rc[:, h:h + 1]
            b_col = rc[:, NH_A + h:NH_A + h + 1]
            i_row = rr[h:h + 1, :]
            b_row = rr[NH_A + h:NH_A + h + 1, :]
            b_tot = b_row[:, L - 1:L]
            m_prev = m_ref[h:h + 1, 0:1]

            s = lax.dot_general(qh, kh.astype(BF16), (((1,), (1,)), ((), ())), preferred_element_type=F32)
            d = jnp.where(causal, b_col - b_row + i_row, -jnp.inf)
            inter = b_col + m_prev
            m_t = jnp.maximum(inter, jnp.max(d, axis=-1, keepdims=True))
            sw = s * jnp.exp(d - m_t)
            w_col = b_tot - b_col + i_col
            w_row = b_tot - b_row + i_row
            m_new = jnp.maximum(b_tot + m_prev, jnp.max(w_row, axis=-1, keepdims=True))
            ks = jnp.exp(w_col - m_new) * kh
            m_ref[h:h + 1, :] = jnp.broadcast_to(m_new, (1, LANES_V7X))
            return dict(lo=lo, h=h, qh=qh, m_t=m_t, e_int=jnp.exp(inter - m_t),
                        sw_sum=jnp.sum(sw, axis=-1, keepdims=True), sw=sw.astype(BF16),
                        decay=jnp.exp(b_tot + m_prev - m_new),
                        ks_sum=jnp.sum(ks, axis=0, keepdims=True), ks=ks.astype(BF16))

        def outputs(a):
            lo, h, qh = a["lo"], a["h"], a["qh"]
            vh = v_all[lo:lo + L, h * DV_A:(h + 1) * DV_A]
            n_prev = n_ref[h:h + 1, :]
            ct = ct_ref[h]
            lhs = jnp.concatenate([a["sw"], (a["e_int"] * qh.astype(F32)).astype(BF16)], axis=1)
            num = jnp.dot(lhs, jnp.concatenate([vh, ct.astype(BF16)], axis=0), preferred_element_type=F32)
            ct_ref[h] = a["decay"] * ct + lax.dot_general(
                a["ks"], vh, (((0,), (0,)), ((), ())), preferred_element_type=F32)
            n_ref[h:h + 1, :] = a["decay"] * n_prev + a["ks_sum"]
            qn = jnp.sum(qh.astype(F32) * n_prev, axis=-1, keepdims=True)
            den = a["sw_sum"] + a["e_int"] * qn
            inv = 1.0 / jnp.maximum(jnp.abs(den), jnp.exp(-a["m_t"]))
            scale = inv * lax.rsqrt(inv * inv * jnp.mean(num * num, axis=-1, keepdims=True) + EPS)
            hv = num * scale * hnorm[:, h * DV_A:(h + 1) * DV_A]
            hv = hv * jax.nn.sigmoid(o_all[lo:lo + L, h * DV_A:(h + 1) * DV_A])
            hg_ref[lo:lo + L, h * DV_A:(h + 1) * DV_A] = hv.astype(BF16)

        pending = None
        for c in range(tm // L):
            for h in range(NH_A):
                a = weights(c, h)
                yield
                if pending is not None:
                    outputs(pending)
                    yield
                pending = a
        outputs(pending)

    def post():
        x1 = x_prev_ref[...] + jnp.dot(hg_prev_ref[...], wout_ref[...], preferred_element_type=F32)
        yield
        x2 = yield from _ffn_steps(x1, fnorm_ref[...], wgu_ref, wd_ref, act_ref, fine=True)
        o_ref[...] = x2
        if emit_kv:
            kvn = _rms(x2, kvnorm_ref[...]).astype(BF16)
            kv = jnp.dot(kvn, wkv_ref[...], preferred_element_type=F32) + bkv_ref[...]
            kv_ref[...] = kv.astype(BF16)

    def keep_for_next_step():
        hg_prev_ref[...] = hg_ref[...]
        x_prev_ref[...] = x_ref[...]

    _pipelined_steps(mixer, post, keep_for_next_step)


def _swa_layer_kernel(*refs, tm, final):
    if final:
        (sink_ref, x_ref, kvc_ref, kvp_ref, bnorm_ref, wq_ref, bq_ref, wo_ref, bo_ref, fnorm_ref, wgu_ref, wd_ref,
         final_ref, o_ref, attn_ref, attn_prev_ref, x_prev_ref, act_ref) = refs
    else:
        (sink_ref, x_ref, kvc_ref, kvp_ref, bnorm_ref, wq_ref, bq_ref, wo_ref, bo_ref, fnorm_ref, wgu_ref, wd_ref,
         o_ref, attn_ref, attn_prev_ref, x_prev_ref, act_ref) = refs

    W = WINDOW
    pairs = GRP_B // 2
    half = pairs * W

    kj = lax.broadcasted_iota(jnp.int32, (2 * W, GRP_B * W), 0)
    qi = lax.broadcasted_iota(jnp.int32, (2 * W, GRP_B * W), 1) & (W - 1)
    band = (kj > qi) & (kj <= qi + W)
    band_first = band & ((kj >= W) | (pl.program_id(0) > 0))
    lane_i = lax.broadcasted_iota(jnp.int32, (2 * W, LANES_V7X), 1)
    low = lane_i < DH_B
    zero = jnp.zeros((2 * W, LANES_V7X), BF16)
    ones_lane_hi = jnp.where(lane_i == DH_B, 1.0, 0.0).astype(BF16)
    ones_lane_lo = jnp.where(lane_i == 0, 1.0, 0.0).astype(BF16)
    nt = (((1,), (1,)), ((), ()))
    tn = (((0,), (0,)), ((), ()))

    def mixer():
        xn = _rms(x_ref[...], bnorm_ref[...]).astype(BF16)
        q = jnp.dot(xn, wq_ref[...], preferred_element_type=F32) + bq_ref[...]
        q = (q * (DH_B ** -0.5 * LOG2_E)).astype(BF16)
        for b in range(tm // W):
            k0 = b * W
            qb = q[b * W:(b + 1) * W]
            kv_prev = kvp_ref[...] if k0 == 0 else kvc_ref[k0 - W:k0, :]
            kvb = jnp.concatenate([kv_prev, kvc_ref[k0:k0 + W, :]], axis=0)
            k_01, v_01, k_10, v_10 = (kvb[:, c * LANES_V7X:(c + 1) * LANES_V7X] for c in range(4))
            valid = band_first if k0 == 0 else band
            for kh in range(KVH_B):
                k_same, k_swap = (k_01, k_10) if kh == 0 else (k_10, k_01)
                v_same, v_swap = (v_01, v_10) if kh == 0 else (v_10, v_01)
                k_even, k_odd = jnp.where(low, k_same, zero), jnp.where(low, zero, k_swap)
                v_even = jnp.where(low, v_same, ones_lane_hi)
                v_odd = jnp.where(low, ones_lane_lo, v_swap)
                qp = jnp.concatenate(
                    [qb[:, (kh * pairs + j) * LANES_V7X:(kh * pairs + j + 1) * LANES_V7X] for j in range(pairs)],
                    axis=0)
                s = jnp.concatenate(
                    [lax.dot_general(k_even, qp, nt, preferred_element_type=F32),
                     lax.dot_general(k_odd, qp, nt, preferred_element_type=F32)], axis=1)
                heads = ([kh * GRP_B + 2 * j for j in range(pairs)]
                         + [kh * GRP_B + 2 * j + 1 for j in range(pairs)])
                sk = jnp.concatenate([jnp.full((1, W), sink_ref[hd] * LOG2_E, F32) for hd in heads], axis=1)
                s = jnp.where(valid, s, -jnp.inf)
                mx = jnp.maximum(jnp.max(s, axis=0, keepdims=True), sk)
                p = jnp.exp2(s - mx).astype(BF16)
                sink_p = jnp.exp2(sk - mx)
                yield
                o_even = lax.dot_general(v_even, p[:, :half], tn, preferred_element_type=F32)
                o_odd = lax.dot_general(v_odd, p[:, half:], tn, preferred_element_type=F32)
                inv_even = 1.0 / (o_even[DH_B:DH_B + 1, :] + sink_p[:, :half])
                inv_odd = 1.0 / (o_odd[0:1, :] + sink_p[:, half:])
                o = jnp.concatenate([o_even[:DH_B] * inv_even, o_odd[DH_B:] * inv_odd], axis=0)
                for j in range(pairs):
                    lo = (kh * pairs + j) * LANES_V7X
                    attn_ref[k0:k0 + W, lo:lo + LANES_V7X] = o[:, j * W:(j + 1) * W].T.astype(BF16)

    def post():
        x1 = (x_prev_ref[...] + jnp.dot(attn_prev_ref[...], wo_ref[...], preferred_element_type=F32)
              + bo_ref[...])
        yield
        x2 = yield from _ffn_steps(x1, fnorm_ref[...], wgu_ref, wd_ref, act_ref)
        if final:
            x2 = _rms(x2, final_ref[...])
        o_ref[...] = x2

    def keep_for_next_step():
        attn_prev_ref[...] = attn_ref[...]
        x_prev_ref[...] = x_ref[...]

    _pipelined_steps(mixer, post, keep_for_next_step)


def _resident(operand):
    if isinstance(operand, tuple):
        stack, layer = operand
        cols = stack.shape[2] // LANES_V7X * LANES_V7X
        return pl.BlockSpec((None, stack.shape[1], cols), lambda i: (layer, 0, 0), pipeline_mode=pl.Buffered(1))
    return pl.BlockSpec(operand.shape, lambda i: (0,) * operand.ndim, pipeline_mode=pl.Buffered(1))


def _mlstm_layer(x, anorm, wqkvo, wg, bg, hnorm, wout, fnorm, wgu, wd, kv_params, *, tm):
    S = x.shape[0]
    emit_kv = kv_params is not None
    n_tiles = S // tm
    operands = [x, anorm, wqkvo, wg, bg, hnorm, wout, fnorm, wgu, wd]
    if emit_kv:
        operands += list(kv_params)
    in_specs = ([pl.BlockSpec((tm, D_MODEL), lambda i: (jnp.minimum(i, n_tiles - 1), 0))]
                + [_resident(a) for a in operands[1:]])
    prev_tile = lambda i: (jnp.maximum(i - 1, 0), 0)
    out_shape = [jax.ShapeDtypeStruct((S, D_MODEL), F32)]
    out_specs = [pl.BlockSpec((tm, D_MODEL), prev_tile)]
    if emit_kv:
        kv_width = kv_params[1].shape[1]
        out_shape.append(jax.ShapeDtypeStruct((S, kv_width), BF16))
        out_specs.append(pl.BlockSpec((tm, kv_width), prev_tile))
    out = pl.pallas_call(
        functools.partial(_mlstm_layer_kernel, tm=tm, emit_kv=emit_kv),
        grid=(n_tiles + 1,),
        in_specs=in_specs,
        out_specs=out_specs,
        out_shape=out_shape,
        scratch_shapes=[
            pltpu.VMEM((NH_A, DQK_A, DV_A), F32),
            pltpu.VMEM((NH_A, DQK_A), F32),
            pltpu.VMEM((NH_A, LANES_V7X), F32),
            pltpu.VMEM((tm, V_W), BF16),
            pltpu.VMEM((tm, V_W), BF16),
            pltpu.VMEM((tm, D_MODEL), F32),
            pltpu.VMEM((tm, D_FF), BF16),
        ],
        compiler_params=pltpu.CompilerParams(
            dimension_semantics=("arbitrary",), vmem_limit_bytes=VMEM_LIMIT_BYTES_V7X),
        name="mlstm_layer_kv" if emit_kv else "mlstm_layer",
    )(*[a[0] if isinstance(a, tuple) else a for a in operands])
    return out if emit_kv else (out[0], None)


def _swa_layer(x, kv, sinks, bnorm, wq, bq, wo, bo, fnorm, wgu, wd, final_norm, *, tm):
    S = x.shape[0]
    final = final_norm is not None
    n_tiles = S // tm
    blocks_per_tile = tm // WINDOW
    operands = [sinks, x, kv, kv, bnorm, wq, bq, wo, bo, fnorm, wgu, wd]
    if final:
        operands.append(final_norm)
    mixer_tile = lambda i: jnp.minimum(i, n_tiles - 1)
    in_specs = [
        pl.BlockSpec(memory_space=pltpu.SMEM),
        pl.BlockSpec((tm, D_MODEL), lambda i: (mixer_tile(i), 0)),
        pl.BlockSpec((tm, kv.shape[1]), lambda i: (mixer_tile(i), 0)),
        pl.BlockSpec((WINDOW, kv.shape[1]), lambda i: (jnp.maximum(mixer_tile(i) * blocks_per_tile - 1, 0), 0)),
    ] + [_resident(a) for a in operands[4:]]
    return pl.pallas_call(
        functools.partial(_swa_layer_kernel, tm=tm, final=final),
        grid=(n_tiles + 1,),
        in_specs=in_specs,
        out_specs=pl.BlockSpec((tm, D_MODEL), lambda i: (jnp.maximum(i - 1, 0), 0)),
        out_shape=jax.ShapeDtypeStruct((S, D_MODEL), F32),
        scratch_shapes=[
            pltpu.VMEM((tm, NH_B * DH_B), BF16),
            pltpu.VMEM((tm, NH_B * DH_B), BF16),
            pltpu.VMEM((tm, D_MODEL), F32),
            pltpu.VMEM((tm, D_FF), BF16),
        ],
        compiler_params=pltpu.CompilerParams(
            dimension_semantics=("arbitrary",), vmem_limit_bytes=VMEM_LIMIT_BYTES_V7X),
        name="swa_layer_final" if final else "swa_layer",
    )(*[a[0] if isinstance(a, tuple) else a for a in operands])


def kernel(x, a_norm, a_w_in, a_b_gates, a_head_norm, a_w_out, kv_norm, w_kv, b_kv, b_norm, b_w_q, b_b_q,
           b_sinks, b_w_out, b_b_out, ffn_norm, w_gate_up, w_down, final_norm):
    bsz, S, _ = x.shape
    assert bsz == 1 and S % ROW_TILE_A == 0 and S % ROW_TILE_B == 0
    n_a = a_w_in.shape[0]
    n_b = b_w_q.shape[0]
    row = lambda v: v.reshape(1, -1).astype(F32)
    n_qkvo = 2 * QK_W + 2 * V_W
    gate_pad = LANES_V7X - 2 * NH_A
    w_in, w_out_a = a_w_in[:, :, :n_qkvo].astype(BF16), a_w_out.astype(BF16)
    w_q, w_out_b = b_w_q.astype(BF16), b_w_out.astype(BF16)
    w_gu, w_d = w_gate_up.astype(BF16), w_down.astype(BF16)

    h = x[0]
    kv = None
    for l in range(n_a):
        wg = jnp.pad(a_w_in[l][:, n_qkvo:], ((0, 0), (0, gate_pad))).astype(BF16)
        bg = jnp.pad(a_b_gates[l], (0, gate_pad)).reshape(1, -1).astype(F32)
        kv_params = None
        if l == n_a - 1:
            kw = KVH_B * DH_B
            order = jnp.array([*range(2 * kw), *range(DH_B, kw), *range(DH_B), *range(kw + DH_B, 2 * kw),
                               *range(kw, kw + DH_B)], jnp.int32)
            kv_params = (row(kv_norm), w_kv[:, order].astype(BF16), row(b_kv[order]))
        h, kv = _mlstm_layer(
            h, row(a_norm[l]), (w_in, l), wg, bg, row(a_head_norm[l]), (w_out_a, l), row(ffn_norm[l]),
            (w_gu, l), (w_d, l), kv_params, tm=ROW_TILE_A)
    for j in range(n_b):
        l = n_a + j
        h = _swa_layer(
            h, kv, b_sinks[j].astype(F32), row(b_norm[j]), (w_q, j), row(b_b_q[j]), (w_out_b, j),
            row(b_b_out[j]), row(ffn_norm[l]), (w_gu, l), (w_d, l),
            row(final_norm) if j == n_b - 1 else None, tm=ROW_TILE_B)
    return h[None]
```

```python
import functools

import jax
import jax.numpy as jnp
from jax import lax
from jax.experimental import pallas as pl
from jax.experimental.pallas import tpu as pltpu

F32 = jnp.float32
BF16 = jnp.bfloat16

D_MODEL = 1024
EPS = 1e-6
NH_A = 4
DV_A = D_MODEL // NH_A
DQK_A = DV_A // 2
GATE_CAP = 15.0
LOG2_E = 1.4426950408889634
QK_W = NH_A * DQK_A
V_W = NH_A * DV_A
NH_B = 16
KVH_B = 2
GRP_B = NH_B // KVH_B
DH_B = 64
WINDOW = 128
D_FF = 2816

LANES_V7X = 128
MXU_DIM_V7X = 256
VMEM_LIMIT_BYTES_V7X = 58 * 1024 * 1024

MLSTM_CHUNK = 256
FFN_CHUNK = MXU_DIM_V7X
ROW_TILE_A = 512
ROW_TILE_B = 512
ROW_TILE_KV = 1024


def _rms(x, g):
    ms = jnp.mean(x * x, axis=-1, keepdims=True)
    return x * lax.rsqrt(ms + EPS) * g


def _ffn_steps(x1, fnorm, wgu_ref, wd_ref, act_ref, fine=False):
    xn = _rms(x1, fnorm).astype(BF16)
    for j in range(D_FF // FFN_CHUNK):
        lo = j * FFN_CHUNK
        g = jnp.dot(xn, wgu_ref[:, lo:lo + FFN_CHUNK], preferred_element_type=F32)
        if fine:
            yield
        u = jnp.dot(xn, wgu_ref[:, D_FF + lo:D_FF + lo + FFN_CHUNK], preferred_element_type=F32)
        act_ref[:, lo:lo + FFN_CHUNK] = (g * jax.nn.sigmoid(g) * u).astype(BF16)
        yield
    return x1 + jnp.dot(act_ref[...], wd_ref[...], preferred_element_type=F32)


def _interleave(main, side):
    live = [main, side]
    while live:
        for gen in list(live):
            try:
                next(gen)
            except StopIteration:
                live.remove(gen)


def _run(gen):
    for _ in gen:
        pass


def _pipelined_steps(mixer, post, keep_for_next_step):
    i = pl.program_id(0)
    last = pl.num_programs(0) - 1

    @pl.when(i == 0)
    def _():
        _run(mixer())

    @pl.when(jnp.logical_and(i > 0, i < last))
    def _():
        _interleave(mixer(), post())

    @pl.when(i == last)
    def _():
        _run(post())

    @pl.when(i < last)
    def _():
        keep_for_next_step()


def _mlstm_layer_kernel(x_ref, anorm_ref, wqkvo_ref, wg_ref, bg_ref, hnorm_ref, wout_ref, fnorm_ref, wgu_ref, wd_ref,
                        o_ref, ct_ref, n_ref, m_ref, hg_ref, hg_prev_ref, x_prev_ref, act_ref, *, tm):
    L = MLSTM_CHUNK

    @pl.when(pl.program_id(0) == 0)
    def _():
        ct_ref[...] = jnp.zeros_like(ct_ref)
        n_ref[...] = jnp.zeros_like(n_ref)
        m_ref[...] = jnp.zeros_like(m_ref)

    def mixer():
        xn = _rms(x_ref[...], anorm_ref[...]).astype(BF16)

        gates = jnp.dot(xn, wg_ref[...], preferred_element_type=F32) + bg_ref[...]
        gates = GATE_CAP * jnp.tanh(gates / GATE_CAP)
        lane = lax.broadcasted_iota(jnp.int32, gates.shape, 1)
        log_sig = jnp.minimum(gates, 0.0) - jnp.log1p(jnp.exp(-jnp.abs(gates)))
        gl = jnp.where(lane < NH_A, gates, log_sig)
        t_loc = lax.broadcasted_iota(jnp.int32, gl.shape, 0) & (L - 1)
        b = gl
        k = 1
        while k < L:
            b = b + jnp.where(t_loc >= k, pltpu.roll(b, k, axis=0), 0.0)
            k *= 2
        r_col = jnp.where(lane < NH_A, gl, b)
        r_row = r_col.T

        q_all = jnp.dot(xn, wqkvo_ref[:, 0:QK_W], preferred_element_type=F32)
        q_all = (q_all * (DQK_A ** -0.5)).astype(BF16)
        k_all = jnp.dot(xn, wqkvo_ref[:, QK_W:2 * QK_W], preferred_element_type=F32)
        v_all = jnp.dot(xn, wqkvo_ref[:, 2 * QK_W:2 * QK_W + V_W], preferred_element_type=F32).astype(BF16)
        o_all = jnp.dot(xn, wqkvo_ref[:, 2 * QK_W + V_W:2 * QK_W + 2 * V_W], preferred_element_type=F32)
        yield

        row_i = lax.broadcasted_iota(jnp.int32, (L, L), 0)
        col_i = lax.broadcasted_iota(jnp.int32, (L, L), 1)
        causal = col_i <= row_i
        hnorm = hnorm_ref[...]

        def weights(c, h):
            lo = c * L
            rc = r_col[lo:lo + L]
            rr = r_row[:, lo:lo + L]
            qh = q_all[lo:lo + L, h * DQK_A:(h + 1) * DQK_A]
            kh = k_all[lo:lo + L, h * DQK_A:(h + 1) * DQK_A]
            i_col = rc[:, h:h + 1]
            b_col = rc[:, NH_A + h:NH_A + h + 1]
            i_row = rr[h:h + 1, :]
            b_row = rr[NH_A + h:NH_A + h + 1, :]
            b_tot = b_row[:, L - 1:L]
            m_prev = m_ref[h:h + 1, 0:1]

            s = lax.dot_general(qh, kh.astype(BF16), (((1,), (1,)), ((), ())), preferred_element_type=F32)
            d = jnp.where(causal, b_col - b_row + i_row, -jnp.inf)
            inter = b_col + m_prev
            m_t = jnp.maximum(inter, jnp.max(d, axis=-1, keepdims=True))
            sw = s * jnp.exp(d - m_t)
            w_col = b_tot - b_col + i_col
            w_row = b_tot - b_row + i_row
            m_new = jnp.maximum(b_tot + m_prev, jnp.max(w_row, axis=-1, keepdims=True))
            ks = jnp.exp(w_col - m_new) * kh
            m_ref[h:h + 1, :] = jnp.broadcast_to(m_new, (1, LANES_V7X))
            return dict(lo=lo, h=h, qh=qh, m_t=m_t, e_int=jnp.exp(inter - m_t),
                        sw_sum=jnp.sum(sw, axis=-1, keepdims=True), sw=sw.astype(BF16),
                        decay=jnp.exp(b_tot + m_prev - m_new),
                        ks_sum=jnp.sum(ks, axis=0, keepdims=True), ks=ks.astype(BF16))

        def outputs(a):
            lo, h, qh = a["lo"], a["h"], a["qh"]
            vh = v_all[lo:lo + L, h * DV_A:(h + 1) * DV_A]
            n_prev = n_ref[h:h + 1, :]
            ct = ct_ref[h]
            lhs = jnp.concatenate([a["sw"], (a["e_int"] * qh.astype(F32)).astype(BF16)], axis=1)
            num = jnp.dot(lhs, jnp.concatenate([vh, ct.astype(BF16)], axis=0), preferred_element_type=F32)
            ct_ref[h] = a["decay"] * ct + lax.dot_general(
                a["ks"], vh, (((0,), (0,)), ((), ())), preferred_element_type=F32)
            n_ref[h:h + 1, :] = a["decay"] * n_prev + a["ks_sum"]
            qn = jnp.sum(qh.astype(F32) * n_prev, axis=-1, keepdims=True)
            den = a["sw_sum"] + a["e_int"] * qn
            inv = 1.0 / jnp.maximum(jnp.abs(den), jnp.exp(-a["m_t"]))
            scale = inv * lax.rsqrt(inv * inv * jnp.mean(num * num, axis=-1, keepdims=True) + EPS)
            hv = num * scale * hnorm[:, h * DV_A:(h + 1) * DV_A]
            hv = hv * jax.nn.sigmoid(o_all[lo:lo + L, h * DV_A:(h + 1) * DV_A])
            hg_ref[lo:lo + L, h * DV_A:(h + 1) * DV_A] = hv.astype(BF16)

        pending = None
        for c in range(tm // L):
            for h in range(NH_A):
                a = weights(c, h)
                yield
                if pending is not None:
                    outputs(pending)
                    yield
                pending = a
        outputs(pending)

    def post():
        x1 = x_prev_ref[...] + jnp.dot(hg_prev_ref[...], wout_ref[...], preferred_element_type=F32)
        yield
        x2 = yield from _ffn_steps(x1, fnorm_ref[...], wgu_ref, wd_ref, act_ref, fine=True)
        o_ref[...] = x2

    def keep_for_next_step():
        hg_prev_ref[...] = hg_ref[...]
        x_prev_ref[...] = x_ref[...]

    _pipelined_steps(mixer, post, keep_for_next_step)


def _shared_kv_kernel(x_ref, norm_ref, w_ref, b_ref, o_ref):
    xn = _rms(x_ref[...], norm_ref[...]).astype(BF16)
    o_ref[...] = (jnp.dot(xn, w_ref[...], preferred_element_type=F32) + b_ref[...]).astype(BF16)


def _swa_layer_kernel(*refs, tm, final):
    if final:
        (sink_ref, x_ref, kvc_ref, kvp_ref, bnorm_ref, wq_ref, bq_ref, wo_ref, bo_ref, fnorm_ref, wgu_ref, wd_ref,
         final_ref, o_ref, attn_ref, attn_prev_ref, x_prev_ref, act_ref) = refs
    else:
        (sink_ref, x_ref, kvc_ref, kvp_ref, bnorm_ref, wq_ref, bq_ref, wo_ref, bo_ref, fnorm_ref, wgu_ref, wd_ref,
         o_ref, attn_ref, attn_prev_ref, x_prev_ref, act_ref) = refs

    W = WINDOW
    pairs = GRP_B // 2
    half = pairs * W

    kj = lax.broadcasted_iota(jnp.int32, (2 * W, GRP_B * W), 0)
    qi = lax.broadcasted_iota(jnp.int32, (2 * W, GRP_B * W), 1) & (W - 1)
    band = (kj > qi) & (kj <= qi + W)
    band_first = band & ((kj >= W) | (pl.program_id(0) > 0))
    lane_i = lax.broadcasted_iota(jnp.int32, (2 * W, LANES_V7X), 1)
    low = lane_i < DH_B
    zero = jnp.zeros((2 * W, LANES_V7X), BF16)
    ones_lane_hi = jnp.where(lane_i == DH_B, 1.0, 0.0).astype(BF16)
    ones_lane_lo = jnp.where(lane_i == 0, 1.0, 0.0).astype(BF16)
    nt = (((1,), (1,)), ((), ()))
    tn = (((0,), (0,)), ((), ()))

    def mixer():
        xn = _rms(x_ref[...], bnorm_ref[...]).astype(BF16)
        q = jnp.dot(xn, wq_ref[...], preferred_element_type=F32) + bq_ref[...]
        q = (q * (DH_B ** -0.5 * LOG2_E)).astype(BF16)
        for b in range(tm // W):
            k0 = b * W
            qb = q[b * W:(b + 1) * W]
            kv_prev = kvp_ref[...] if k0 == 0 else kvc_ref[k0 - W:k0, :]
            kvb = jnp.concatenate([kv_prev, kvc_ref[k0:k0 + W, :]], axis=0)
            k_01, v_01, k_10, v_10 = (kvb[:, c * LANES_V7X:(c + 1) * LANES_V7X] for c in range(4))
            valid = band_first if k0 == 0 else band
            for kh in range(KVH_B):
                k_same, k_swap = (k_01, k_10) if kh == 0 else (k_10, k_01)
                v_same, v_swap = (v_01, v_10) if kh == 0 else (v_10, v_01)
                k_even, k_odd = jnp.where(low, k_same, zero), jnp.where(low, zero, k_swap)
                v_even = jnp.where(low, v_same, ones_lane_hi)
                v_odd = jnp.where(low, ones_lane_lo, v_swap)
                qp = jnp.concatenate(
                    [qb[:, (kh * pairs + j) * LANES_V7X:(kh * pairs + j + 1) * LANES_V7X] for j in range(pairs)],
                    axis=0)
                s = jnp.concatenate(
                    [lax.dot_general(k_even, qp, nt, preferred_element_type=F32),
                     lax.dot_general(k_odd, qp, nt, preferred_element_type=F32)], axis=1)
                heads = ([kh * GRP_B + 2 * j for j in range(pairs)]
                         + [kh * GRP_B + 2 * j + 1 for j in range(pairs)])
                sk = jnp.concatenate([jnp.full((1, W), sink_ref[hd] * LOG2_E, F32) for hd in heads], axis=1)
                s = jnp.where(valid, s, -jnp.inf)
                mx = jnp.maximum(jnp.max(s, axis=0, keepdims=True), sk)
                p = jnp.exp2(s - mx).astype(BF16)
                sink_p = jnp.exp2(sk - mx)
                yield
                o_even = lax.dot_general(v_even, p[:, :half], tn, preferred_element_type=F32)
                o_odd = lax.dot_general(v_odd, p[:, half:], tn, preferred_element_type=F32)
                inv_even = 1.0 / (o_even[DH_B:DH_B + 1, :] + sink_p[:, :half])
                inv_odd = 1.0 / (o_odd[0:1, :] + sink_p[:, half:])
                o = jnp.concatenate([o_even[:DH_B] * inv_even, o_odd[DH_B:] * inv_odd], axis=0)
                for j in range(pairs):
                    lo = (kh * pairs + j) * LANES_V7X
                    attn_ref[k0:k0 + W, lo:lo + LANES_V7X] = o[:, j * W:(j + 1) * W].T.astype(BF16)

    def post():
        x1 = (x_prev_ref[...] + jnp.dot(attn_prev_ref[...], wo_ref[...], preferred_element_type=F32)
              + bo_ref[...])
        yield
        x2 = yield from _ffn_steps(x1, fnorm_ref[...], wgu_ref, wd_ref, act_ref)
        if final:
            x2 = _rms(x2, final_ref[...])
        o_ref[...] = x2

    def keep_for_next_step():
        attn_prev_ref[...] = attn_ref[...]
        x_prev_ref[...] = x_ref[...]

    _pipelined_steps(mixer, post, keep_for_next_step)


def _resident(operand):
    if isinstance(operand, tuple):
        stack, layer = operand
        cols = stack.shape[2] // LANES_V7X * LANES_V7X
        return pl.BlockSpec((None, stack.shape[1], cols), lambda i: (layer, 0, 0), pipeline_mode=pl.Buffered(1))
    return pl.BlockSpec(operand.shape, lambda i: (0,) * operand.ndim, pipeline_mode=pl.Buffered(1))


def _mlstm_layer(x, anorm, wqkvo, wg, bg, hnorm, wout, fnorm, wgu, wd, *, tm):
    S = x.shape[0]
    n_tiles = S // tm
    operands = [x, anorm, wqkvo, wg, bg, hnorm, wout, fnorm, wgu, wd]
    in_specs = ([pl.BlockSpec((tm, D_MODEL), lambda i: (jnp.minimum(i, n_tiles - 1), 0))]
                + [_resident(a) for a in operands[1:]])
    return pl.pallas_call(
        functools.partial(_mlstm_layer_kernel, tm=tm),
        grid=(n_tiles + 1,),
        in_specs=in_specs,
        out_specs=pl.BlockSpec((tm, D_MODEL), lambda i: (jnp.maximum(i - 1, 0), 0)),
        out_shape=jax.ShapeDtypeStruct((S, D_MODEL), F32),
        scratch_shapes=[
            pltpu.VMEM((NH_A, DQK_A, DV_A), F32),
            pltpu.VMEM((NH_A, DQK_A), F32),
            pltpu.VMEM((NH_A, LANES_V7X), F32),
            pltpu.VMEM((tm, V_W), BF16),
            pltpu.VMEM((tm, V_W), BF16),
            pltpu.VMEM((tm, D_MODEL), F32),
            pltpu.VMEM((tm, D_FF), BF16),
        ],
        compiler_params=pltpu.CompilerParams(
            dimension_semantics=("arbitrary",), vmem_limit_bytes=VMEM_LIMIT_BYTES_V7X),
        name="mlstm_layer",
    )(*[a[0] if isinstance(a, tuple) else a for a in operands])


def _shared_kv(x, norm, w, b, *, tm):
    S = x.shape[0]
    return pl.pallas_call(
        _shared_kv_kernel,
        grid=(S // tm,),
        in_specs=[pl.BlockSpec((tm, D_MODEL), lambda i: (i, 0)), _resident(norm), _resident(w), _resident(b)],
        out_specs=pl.BlockSpec((tm, w.shape[1]), lambda i: (i, 0)),
        out_shape=jax.ShapeDtypeStruct((S, w.shape[1]), BF16),
        compiler_params=pltpu.CompilerParams(
            dimension_semantics=("arbitrary",), vmem_limit_bytes=VMEM_LIMIT_BYTES_V7X),
        name="shared_kv",
    )(x, norm, w, b)


def _swa_layer(x, kv, sinks, bnorm, wq, bq, wo, bo, fnorm, wgu, wd, final_norm, *, tm):
    S = x.shape[0]
    final = final_norm is not None
    n_tiles = S // tm
    blocks_per_tile = tm // WINDOW
    operands = [sinks, x, kv, kv, bnorm, wq, bq, wo, bo, fnorm, wgu, wd]
    if final:
        operands.append(final_norm)
    mixer_tile = lambda i: jnp.minimum(i, n_tiles - 1)
    in_specs = [
        pl.BlockSpec(memory_space=pltpu.SMEM),
        pl.BlockSpec((tm, D_MODEL), lambda i: (mixer_tile(i), 0)),
        pl.BlockSpec((tm, kv.shape[1]), lambda i: (mixer_tile(i), 0)),
        pl.BlockSpec((WINDOW, kv.shape[1]), lambda i: (jnp.maximum(mixer_tile(i) * blocks_per_tile - 1, 0), 0)),
    ] + [_resident(a) for a in operands[4:]]
    return pl.pallas_call(
        functools.partial(_swa_layer_kernel, tm=tm, final=final),
        grid=(n_tiles + 1,),
        in_specs=in_specs,
        out_specs=pl.BlockSpec((tm, D_MODEL), lambda i: (jnp.maximum(i - 1, 0), 0)),
        out_shape=jax.ShapeDtypeStruct((S, D_MODEL), F32),
        scratch_shapes=[
            pltpu.VMEM((tm, NH_B * DH_B), BF16),
            pltpu.VMEM((tm, NH_B * DH_B), BF16),
            pltpu.VMEM((tm, D_MODEL), F32),
            pltpu.VMEM((tm, D_FF), BF16),
        ],
        compiler_params=pltpu.CompilerParams(
            dimension_semantics=("arbitrary",), vmem_limit_bytes=VMEM_LIMIT_BYTES_V7X),
        name="swa_layer_final" if final else "swa_layer",
    )(*[a[0] if isinstance(a, tuple) else a for a in operands])


def kernel(x, a_norm, a_w_in, a_b_gates, a_head_norm, a_w_out, kv_norm, w_kv, b_kv, b_norm, b_w_q, b_b_q,
           b_sinks, b_w_out, b_b_out, ffn_norm, w_gate_up, w_down, final_norm):
    bsz, S, _ = x.shape
    assert bsz == 1 and S % ROW_TILE_A == 0 and S % ROW_TILE_B == 0 and S % ROW_TILE_KV == 0
    n_a = a_w_in.shape[0]
    n_b = b_w_q.shape[0]
    row = lambda v: v.reshape(1, -1).astype(F32)
    n_qkvo = 2 * QK_W + 2 * V_W
    gate_pad = LANES_V7X - 2 * NH_A
    w_in, w_out_a = a_w_in[:, :, :n_qkvo].astype(BF16), a_w_out.astype(BF16)
    w_q, w_out_b = b_w_q.astype(BF16), b_w_out.astype(BF16)
    w_gu, w_d = w_gate_up.astype(BF16), w_down.astype(BF16)

    h = x[0]
    for l in range(n_a):
        wg = jnp.pad(a_w_in[l][:, n_qkvo:], ((0, 0), (0, gate_pad))).astype(BF16)
        bg = jnp.pad(a_b_gates[l], (0, gate_pad)).reshape(1, -1).astype(F32)
        h = _mlstm_layer(
            h, row(a_norm[l]), (w_in, l), wg, bg, row(a_head_norm[l]), (w_out_a, l), row(ffn_norm[l]),
            (w_gu, l), (w_d, l), tm=ROW_TILE_A)
    kw = KVH_B * DH_B
    order = jnp.array([*range(2 * kw), *range(DH_B, kw), *range(DH_B), *range(kw + DH_B, 2 * kw),
                       *range(kw, kw + DH_B)], jnp.int32)
    kv = _shared_kv(h, row(kv_norm), w_kv[:, order].astype(BF16), row(b_kv[order]), tm=ROW_TILE_KV)
    for j in range(n_b):
        l = n_a + j
        h = _swa_layer(
            h, kv, b_sinks[j].astype(F32), row(b_norm[j]), (w_q, j), row(b_b_q[j]), (w_out_b, j),
            row(b_b_out[j]), row(ffn_norm[l]), (w_gu, l), (w_d, l),
            row(final_norm) if j == n_b - 1 else None, tm=ROW_TILE_B)
    return h[None]
```

```python
import functools

import jax
import jax.numpy as jnp
from jax import lax
from jax.experimental import pallas as pl
from jax.experimental.pallas import tpu as pltpu

F32 = jnp.float32
BF16 = jnp.bfloat16

D_MODEL = 1024
EPS = 1e-6
NH_A = 4
DV_A = D_MODEL // NH_A
DQK_A = DV_A // 2
GATE_CAP = 15.0
LOG2_E = 1.4426950408889634
QK_W = NH_A * DQK_A
V_W = NH_A * DV_A
NH_B = 16
KVH_B = 2
GRP_B = NH_B // KVH_B
DH_B = 64
WINDOW = 128
D_FF = 2816

LANES_V7X = 128
MXU_DIM_V7X = 256
VMEM_LIMIT_BYTES_V7X = 58 * 1024 * 1024

MLSTM_CHUNK = 256
FFN_CHUNK = MXU_DIM_V7X
ROW_TILE_A = 512
ROW_TILE_B = 512
ROW_TILE_KV = 1024


def _rms(x, g):
    ms = jnp.mean(x * x, axis=-1, keepdims=True)
    return x * lax.rsqrt(ms + EPS) * g


def _ffn_steps(x1, fnorm, wgu_ref, wd_ref, act_ref, fine=False):
    xn = _rms(x1, fnorm).astype(BF16)
    for j in range(D_FF // FFN_CHUNK):
        lo = j * FFN_CHUNK
        g = jnp.dot(xn, wgu_ref[:, lo:lo + FFN_CHUNK], preferred_element_type=F32)
        if fine:
            yield
        u = jnp.dot(xn, wgu_ref[:, D_FF + lo:D_FF + lo + FFN_CHUNK], preferred_element_type=F32)
        act_ref[:, lo:lo + FFN_CHUNK] = (g * jax.nn.sigmoid(g) * u).astype(BF16)
        yield
    return x1 + jnp.dot(act_ref[...], wd_ref[...], preferred_element_type=F32)


def _interleave(main, side):
    live = [main, side]
    while live:
        for gen in list(live):
            try:
                next(gen)
            except StopIteration:
                live.remove(gen)


def _run(gen):
    for _ in gen:
        pass


def _pipelined_steps(mixer, post, keep_for_next_step):
    i = pl.program_id(0)
    last = pl.num_programs(0) - 1

    @pl.when(i == 0)
    def _():
        _run(mixer())

    @pl.when(jnp.logical_and(i > 0, i < last))
    def _():
        _interleave(mixer(), post())

    @pl.when(i == last)
    def _():
        _run(post())

    @pl.when(i < last)
    def _():
        keep_for_next_step()


def _mlstm_layer_kernel(x_ref, anorm_ref, wqkvo_ref, wg_ref, bg_ref, hnorm_ref, wout_ref, fnorm_ref, wgu_ref, wd_ref,
                        o_ref, ct_ref, n_ref, m_ref, hg_ref, hg_prev_ref, x_prev_ref, act_ref, *, tm):
    L = MLSTM_CHUNK

    @pl.when(pl.program_id(0) == 0)
    def _():
        ct_ref[...] = jnp.zeros_like(ct_ref)
        n_ref[...] = jnp.zeros_like(n_ref)
        m_ref[...] = jnp.zeros_like(m_ref)

    def mixer():
        xn = _rms(x_ref[...], anorm_ref[...]).astype(BF16)

        gates = jnp.dot(xn, wg_ref[...], preferred_element_type=F32) + bg_ref[...]
        gates = GATE_CAP * jnp.tanh(gates / GATE_CAP)
        lane = lax.broadcasted_iota(jnp.int32, gates.shape, 1)
        log_sig = jnp.minimum(gates, 0.0) - jnp.log1p(jnp.exp(-jnp.abs(gates)))
        gl = jnp.where(lane < NH_A, gates, log_sig)
        t_loc = lax.broadcasted_iota(jnp.int32, gl.shape, 0) & (L - 1)
        b = gl
        k = 1
        while k < L:
            b = b + jnp.where(t_loc >= k, pltpu.roll(b, k, axis=0), 0.0)
            k *= 2
        r_col = jnp.where(lane < NH_A, gl, b)
        r_row = r_col.T

        q_all = jnp.dot(xn, wqkvo_ref[:, 0:QK_W], preferred_element_type=F32)
        q_all = (q_all * (DQK_A ** -0.5)).astype(BF16)
        k_all = jnp.dot(xn, wqkvo_ref[:, QK_W:2 * QK_W], preferred_element_type=F32)
        v_all = jnp.dot(xn, wqkvo_ref[:, 2 * QK_W:2 * QK_W + V_W], preferred_element_type=F32).astype(BF16)
        o_all = jnp.dot(xn, wqkvo_ref[:, 2 * QK_W + V_W:2 * QK_W + 2 * V_W], preferred_element_type=F32)
        yield

        row_i = lax.broadcasted_iota(jnp.int32, (L, L), 0)
        col_i = lax.broadcasted_iota(jnp.int32, (L, L), 1)
        causal = col_i <= row_i
        hnorm = hnorm_ref[...]

        def weights(c, h):
            lo = c * L
            rc = r_col[lo:lo + L]
            rr = r_row[:, lo:lo + L]
            qh = q_all[lo:lo + L, h * DQK_A:(h + 1) * DQK_A]
            kh = k_all[lo:lo + L, h * DQK_A:(h + 1) * DQK_A]
            i_col = rc[:, h:h + 1]
            b_col = rc[:, NH_A + h:NH_A + h + 1]
            i_row = rr[h:h + 1, :]
            b_row = rr[NH_A + h:NH_A + h + 1, :]
            b_tot = b_row[:, L - 1:L]
            m_prev = m_ref[h:h + 1, 0:1]

            s = lax.dot_general(qh, kh.astype(BF16), (((1,), (1,)), ((), ())), preferred_element_type=F32)
            d = jnp.where(causal, b_col - b_row + i_row, -jnp.inf)
            inter = b_col + m_prev
            m_t = jnp.maximum(inter, jnp.max(d, axis=-1, keepdims=True))
            sw = s * jnp.exp(d - m_t)
            w_col = b_tot - b_col + i_col
            w_row = b_tot - b_row + i_row
            m_new = jnp.maximum(b_tot + m_prev, jnp.max(w_row, axis=-1, keepdims=True))
            ks = jnp.exp(w_col - m_new) * kh
            m_ref[h:h + 1, :] = jnp.broadcast_to(m_new, (1, LANES_V7X))
            return dict(lo=lo, h=h, qh=qh, m_t=m_t, e_int=jnp.exp(inter - m_t),
                        sw_sum=jnp.sum(sw, axis=-1, keepdims=True), sw=sw.astype(BF16),
                        decay=jnp.exp(b_tot + m_prev - m_new),
                        ks_sum=jnp.sum(ks, axis=0, keepdims=True), ks=ks.astype(BF16))

        def outputs(a):
            lo, h, qh = a["lo"], a["h"], a["qh"]
            vh = v_all[lo:lo + L, h * DV_A:(h + 1) * DV_A]
            n_prev = n_ref[h:h + 1, :]
            ct = ct_ref[h]
            lhs = jnp.concatenate([a["sw"], (a["e_int"] * qh.astype(F32)).astype(BF16)], axis=1)
            num = jnp.dot(lhs, jnp.concatenate([vh, ct.astype(BF16)], axis=0), preferred_element_type=F32)
            ct_ref[h] = a["decay"] * ct + lax.dot_general(
                a["ks"], vh, (((0,), (0,)), ((), ())), preferred_element_type=F32)
            n_ref[h:h + 1, :] = a["decay"] * n_prev + a["ks_sum"]
            qn = jnp.sum(qh.astype(F32) * n_prev, axis=-1, keepdims=True)
            den = a["sw_sum"] + a["e_int"] * qn
            inv = 1.0 / jnp.maximum(jnp.abs(den), jnp.exp(-a["m_t"]))
            scale = inv * lax.rsqrt(inv * inv * jnp.mean(num * num, axis=-1, keepdims=True) + EPS)
            hv = num * scale * hnorm[:, h * DV_A:(h + 1) * DV_A]
            hv = hv * jax.nn.sigmoid(o_all[lo:lo + L, h * DV_A:(h + 1) * DV_A])
            hg_ref[lo:lo + L, h * DV_A:(h + 1) * DV_A] = hv.astype(BF16)

        pending = None
        for c in range(tm // L):
            for h in range(NH_A):
                a = weights(c, h)
                yield
                if pending is not None:
                    outputs(pending)
                    yield
                pending = a
        outputs(pending)

    def post():
        x1 = x_prev_ref[...] + jnp.dot(hg_prev_ref[...], wout_ref[...], preferred_element_type=F32)
        yield
        x2 = yield from _ffn_steps(x1, fnorm_ref[...], wgu_ref, wd_ref, act_ref, fine=True)
        o_ref[...] = x2

    def keep_for_next_step():
        hg_prev_ref[...] = hg_ref[...]
        x_prev_ref[...] = x_ref[...]

    _pipelined_steps(mixer, post, keep_for_next_step)


def _shared_kv_kernel(x_ref, norm_ref, w_ref, b_ref, o_ref):
    xn = _rms(x_ref[...], norm_ref[...]).astype(BF16)
    o_ref[...] = (jnp.dot(xn, w_ref[...], preferred_element_type=F32) + b_ref[...]).astype(BF16)


def _swa_layer_kernel(*refs, tm, final):
    if final:
        (sink_ref, x_ref, kvc_ref, kvp_ref, bnorm_ref, wq_ref, bq_ref, wo_ref, bo_ref, fnorm_ref, wgu_ref, wd_ref,
         final_ref, o_ref, attn_ref, attn_prev_ref, x_prev_ref, act_ref) = refs
    else:
        (sink_ref, x_ref, kvc_ref, kvp_ref, bnorm_ref, wq_ref, bq_ref, wo_ref, bo_ref, fnorm_ref, wgu_ref, wd_ref,
         o_ref, attn_ref, attn_prev_ref, x_prev_ref, act_ref) = refs

    W = WINDOW
    pairs = GRP_B // 2
    half = pairs * W

    kj = lax.broadcasted_iota(jnp.int32, (2 * W, GRP_B * W), 0)
    qi = lax.broadcasted_iota(jnp.int32, (2 * W, GRP_B * W), 1) & (W - 1)
    band = (kj > qi) & (kj <= qi + W)
    band_first = band & ((kj >= W) | (pl.program_id(0) > 0))
    lane_i = lax.broadcasted_iota(jnp.int32, (2 * W, LANES_V7X), 1)
    low = lane_i < DH_B
    zero = jnp.zeros((2 * W, LANES_V7X), BF16)
    ones_lane_hi = jnp.where(lane_i == DH_B, 1.0, 0.0).astype(BF16)
    ones_lane_lo = jnp.where(lane_i == 0, 1.0, 0.0).astype(BF16)
    nt = (((1,), (1,)), ((), ()))
    tn = (((0,), (0,)), ((), ()))

    def mixer():
        xn = _rms(x_ref[...], bnorm_ref[...]).astype(BF16)
        q = jnp.dot(xn, wq_ref[...], preferred_element_type=F32) + bq_ref[...]
        q = (q * (DH_B ** -0.5 * LOG2_E)).astype(BF16)
        for b in range(tm // W):
            k0 = b * W
            qb = q[b * W:(b + 1) * W]
            kv_prev = kvp_ref[...] if k0 == 0 else kvc_ref[k0 - W:k0, :]
            kvb = jnp.concatenate([kv_prev, kvc_ref[k0:k0 + W, :]], axis=0)
            k_01, v_01, k_10, v_10 = (kvb[:, c * LANES_V7X:(c + 1) * LANES_V7X] for c in range(4))
            valid = band_first if k0 == 0 else band
            for kh in range(KVH_B):
                k_same, k_swap = (k_01, k_10) if kh == 0 else (k_10, k_01)
                v_same, v_swap = (v_01, v_10) if kh == 0 else (v_10, v_01)
                k_even, k_odd = jnp.where(low, k_same, zero), jnp.where(low, zero, k_swap)
                v_even = jnp.where(low, v_same, ones_lane_hi)
                v_odd = jnp.where(low, ones_lane_lo, v_swap)
                qp = jnp.concatenate(
                    [qb[:, (kh * pairs + j) * LANES_V7X:(kh * pairs + j + 1) * LANES_V7X] for j in range(pairs)],
                    axis=0)
                s = jnp.concatenate(
                    [lax.dot_general(k_even, qp, nt, preferred_element_type=F32),
                     lax.dot_general(k_odd, qp, nt, preferred_element_type=F32)], axis=1)
                heads = ([kh * GRP_B + 2 * j for j in range(pairs)]
                         + [kh * GRP_B + 2 * j + 1 for j in range(pairs)])
                sk = jnp.concatenate([jnp.full((1, W), sink_ref[hd] * LOG2_E, F32) for hd in heads], axis=1)
                s = jnp.where(valid, s, -jnp.inf)
                mx = jnp.maximum(jnp.max(s, axis=0, keepdims=True), sk)
                p = jnp.exp2(s - mx).astype(BF16)
                sink_p = jnp.exp2(sk - mx)
                yield
                o_even = lax.dot_general(v_even, p[:, :half], tn, preferred_element_type=F32)
                o_odd = lax.dot_general(v_odd, p[:, half:], tn, preferred_element_type=F32)
                inv_even = 1.0 / (o_even[DH_B:DH_B + 1, :] + sink_p[:, :half])
                inv_odd = 1.0 / (o_odd[0:1, :] + sink_p[:, half:])
                o = jnp.concatenate([o_even[:DH_B] * inv_even, o_odd[DH_B:] * inv_odd], axis=0)
                for j in range(pairs):
                    lo = (kh * pairs + j) * LANES_V7X
                    attn_ref[k0:k0 + W, lo:lo + LANES_V7X] = o[:, j * W:(j + 1) * W].T.astype(BF16)

    def post():
        x1 = (x_prev_ref[...] + jnp.dot(attn_prev_ref[...], wo_ref[...], preferred_element_type=F32)
              + bo_ref[...])
        yield
        x2 = yield from _ffn_steps(x1, fnorm_ref[...], wgu_ref, wd_ref, act_ref)
        if final:
            x2 = _rms(x2, final_ref[...])
        o_ref[...] = x2

    def keep_for_next_step():
        attn_prev_ref[...] = attn_ref[...]
        x_prev_ref[...] = x_ref[...]

    _pipelined_steps(mixer, post, keep_for_next_step)


def _resident(operand):
    if isinstance(operand, tuple):
        stack, layer = operand
        cols = stack.shape[2] // LANES_V7X * LANES_V7X
        return pl.BlockSpec((None, stack.shape[1], cols), lambda i: (layer, 0, 0), pipeline_mode=pl.Buffered(1))
    return pl.BlockSpec(operand.shape, lambda i: (0,) * operand.ndim, pipeline_mode=pl.Buffered(1))


def _mlstm_layer(x, anorm, wqkvo, wg, bg, hnorm, wout, fnorm, wgu, wd, *, tm):
    S = x.shape[0]
    n_tiles = S // tm
    operands = [x, anorm, wqkvo, wg, bg, hnorm, wout, fnorm, wgu, wd]
    in_specs = ([pl.BlockSpec((tm, D_MODEL), lambda i: (jnp.minimum(i, n_tiles - 1), 0))]
                + [_resident(a) for a in operands[1:]])
    return pl.pallas_call(
        functools.partial(_mlstm_layer_kernel, tm=tm),
        grid=(n_tiles + 1,),
        in_specs=in_specs,
        out_specs=pl.BlockSpec((tm, D_MODEL), lambda i: (jnp.maximum(i - 1, 0), 0)),
        out_shape=jax.ShapeDtypeStruct((S, D_MODEL), F32),
        scratch_shapes=[
            pltpu.VMEM((NH_A, DQK_A, DV_A), F32),
            pltpu.VMEM((NH_A, DQK_A), F32),
            pltpu.VMEM((NH_A, LANES_V7X), F32),
            pltpu.VMEM((tm, V_W), BF16),
            pltpu.VMEM((tm, V_W), BF16),
            pltpu.VMEM((tm, D_MODEL), F32),
            pltpu.VMEM((tm, D_FF), BF16),
        ],
        compiler_params=pltpu.CompilerParams(
            dimension_semantics=("arbitrary",), vmem_limit_bytes=VMEM_LIMIT_BYTES_V7X),
        name="mlstm_layer",
    )(*[a[0] if isinstance(a, tuple) else a for a in operands])


def _shared_kv(x, norm, w, b, *, tm):
    S = x.shape[0]
    return pl.pallas_call(
        _shared_kv_kernel,
        grid=(S // tm,),
        in_specs=[pl.BlockSpec((tm, D_MODEL), lambda i: (i, 0)), _resident(norm), _resident(w), _resident(b)],
        out_specs=pl.BlockSpec((tm, w.shape[1]), lambda i: (i, 0)),
        out_shape=jax.ShapeDtypeStruct((S, w.shape[1]), BF16),
        compiler_params=pltpu.CompilerParams(
            dimension_semantics=("arbitrary",), vmem_limit_bytes=VMEM_LIMIT_BYTES_V7X),
        name="shared_kv",
    )(x, norm, w, b)


def _swa_layer(x, kv, sinks, bnorm, wq, bq, wo, bo, fnorm, wgu, wd, final_norm, *, tm):
    S = x.shape[0]
    final = final_norm is not None
    n_tiles = S // tm
    blocks_per_tile = tm // WINDOW
    operands = [sinks, x, kv, kv, bnorm, wq, bq, wo, bo, fnorm, wgu, wd]
    if final:
        operands.append(final_norm)
    mixer_tile = lambda i: jnp.minimum(i, n_tiles - 1)
    in_specs = [
        pl.BlockSpec(memory_space=pltpu.SMEM),
        pl.BlockSpec((tm, D_MODEL), lambda i: (mixer_tile(i), 0)),
        pl.BlockSpec((tm, kv.shape[1]), lambda i: (mixer_tile(i), 0)),
        pl.BlockSpec((WINDOW, kv.shape[1]), lambda i: (jnp.maximum(mixer_tile(i) * blocks_per_tile - 1, 0), 0)),
    ] + [_resident(a) for a in operands[4:]]
    return pl.pallas_call(
        functools.partial(_swa_layer_kernel, tm=tm, final=final),
        grid=(n_tiles + 1,),
        in_specs=in_specs,
        out_specs=pl.BlockSpec((tm, D_MODEL), lambda i: (jnp.maximum(i - 1, 0), 0)),
        out_shape=jax.ShapeDtypeStruct((S, D_MODEL), F32),
        scratch_shapes=[
            pltpu.VMEM((tm, NH_B * DH_B), BF16),
            pltpu.VMEM((tm, NH_B * DH_B), BF16),
            pltpu.VMEM((tm, D_MODEL), F32),
            pltpu.VMEM((tm, D_FF), BF16),
        ],
        compiler_params=pltpu.CompilerParams(
            dimension_semantics=("arbitrary",), vmem_limit_bytes=VMEM_LIMIT_BYTES_V7X),
        name="swa_layer_final" if final else "swa_layer",
    )(*[a[0] if isinstance(a, tuple) else a for a in operands])


def kernel(x, a_norm, a_w_in, a_b_gates, a_head_norm, a_w_out, kv_norm, w_kv, b_kv, b_norm, b_w_q, b_b_q,
           b_sinks, b_w_out, b_b_out, ffn_norm, w_gate_up, w_down, final_norm):
    bsz, S, _ = x.shape
    assert bsz == 1 and S % ROW_TILE_A == 0 and S % ROW_TILE_B == 0 and S % ROW_TILE_KV == 0
    n_a = a_w_in.shape[0]
    n_b = b_w_q.shape[0]
    row = lambda v: v.reshape(1, -1).astype(F32)
    n_qkvo = 2 * QK_W + 2 * V_W
    gate_pad = LANES_V7X - 2 * NH_A
    w_in, w_out_a = a_w_in.astype(BF16), a_w_out.astype(BF16)
    w_q, w_out_b = b_w_q.astype(BF16), b_w_out.astype(BF16)
    w_gu, w_d = w_gate_up.astype(BF16), w_down.astype(BF16)

    h = x[0]
    for l in range(n_a):
        wg = jnp.pad(a_w_in[l][:, n_qkvo:], ((0, 0), (0, gate_pad))).astype(BF16)
        bg = jnp.pad(a_b_gates[l], (0, gate_pad)).reshape(1, -1).astype(F32)
        h = _mlstm_layer(
            h, row(a_norm[l]), (w_in, l), wg, bg, row(a_head_norm[l]), (w_out_a, l), row(ffn_norm[l]),
            (w_gu, l), (w_d, l), tm=ROW_TILE_A)
    kw = KVH_B * DH_B
    order = jnp.array([*range(2 * kw), *range(DH_B, kw), *range(DH_B), *range(kw + DH_B, 2 * kw),
                       *range(kw, kw + DH_B)], jnp.int32)
    kv = _shared_kv(h, row(kv_norm), w_kv[:, order].astype(BF16), row(b_kv[order]), tm=ROW_TILE_KV)
    for j in range(n_b):
        l = n_a + j
        h = _swa_layer(
            h, kv, b_sinks[j].astype(F32), row(b_norm[j]), (w_q, j), row(b_b_q[j]), (w_out_b, j),
            row(b_b_out[j]), row(ffn_norm[l]), (w_gu, l), (w_d, l),
            row(final_norm) if j == n_b - 1 else None, tm=ROW_TILE_B)
    return h[None]
```

```python
import functools

import jax
import jax.numpy as jnp
from jax import lax
from jax.experimental import pallas as pl
from jax.experimental.pallas import tpu as pltpu

F32 = jnp.float32
BF16 = jnp.bfloat16

D_MODEL = 1024
EPS = 1e-6
NH_A = 4
DV_A = D_MODEL // NH_A
DQK_A = DV_A // 2
GATE_CAP = 15.0
LOG2_E = 1.4426950408889634
QK_W = NH_A * DQK_A
V_W = NH_A * DV_A
NH_B = 16
KVH_B = 2
GRP_B = NH_B // KVH_B
DH_B = 64
WINDOW = 128
D_FF = 2816

LANES_V7X = 128
MXU_DIM_V7X = 256
VMEM_LIMIT_BYTES_V7X = 58 * 1024 * 1024

MLSTM_CHUNK = 256
FFN_CHUNK = MXU_DIM_V7X
ROW_TILE_A = 512
ROW_TILE_B = 512
ROW_TILE_KV = 1024


def _rms(x, g):
    ms = jnp.mean(x * x, axis=-1, keepdims=True)
    return x * lax.rsqrt(ms + EPS) * g


def _ffn_steps(x1, fnorm, wgu_ref, wd_ref, act_ref, fine=False):
    xn = _rms(x1, fnorm).astype(BF16)
    for j in range(D_FF // FFN_CHUNK):
        lo = j * FFN_CHUNK
        g = jnp.dot(xn, wgu_ref[:, lo:lo + FFN_CHUNK], preferred_element_type=F32)
        if fine:
            yield
        u = jnp.dot(xn, wgu_ref[:, D_FF + lo:D_FF + lo + FFN_CHUNK], preferred_element_type=F32)
        act_ref[:, lo:lo + FFN_CHUNK] = (g * jax.nn.sigmoid(g) * u).astype(BF16)
        yield
    return x1 + jnp.dot(act_ref[...], wd_ref[...], preferred_element_type=F32)


def _interleave(main, side):
    live = [main, side]
    while live:
        for gen in list(live):
            try:
                next(gen)
            except StopIteration:
                live.remove(gen)


def _run(gen):
    for _ in gen:
        pass


def _pipelined_steps(mixer, post, keep_for_next_step):
    i = pl.program_id(0)
    last = pl.num_programs(0) - 1

    @pl.when(i == 0)
    def _():
        _run(mixer())

    @pl.when(jnp.logical_and(i > 0, i < last))
    def _():
        _interleave(mixer(), post())

    @pl.when(i == last)
    def _():
        _run(post())

    @pl.when(i < last)
    def _():
        keep_for_next_step()


def _mlstm_layer_kernel(x_ref, anorm_ref, wqkvo_ref, wg_ref, bg_ref, hnorm_ref, wout_ref, fnorm_ref, wgu_ref, wd_ref,
                        o_ref, ct_ref, n_ref, m_ref, hg_ref, hg_prev_ref, x_prev_ref, act_ref, *, tm):
    L = MLSTM_CHUNK

    @pl.when(pl.program_id(0) == 0)
    def _():
        ct_ref[...] = jnp.zeros_like(ct_ref)
        n_ref[...] = jnp.zeros_like(n_ref)
        m_ref[...] = jnp.zeros_like(m_ref)

    def mixer():
        xn = _rms(x_ref[...], anorm_ref[...]).astype(BF16)

        gates = jnp.dot(xn, wg_ref[...], preferred_element_type=F32) + bg_ref[...]
        gates = GATE_CAP * jnp.tanh(gates / GATE_CAP)
        lane = lax.broadcasted_iota(jnp.int32, gates.shape, 1)
        log_sig = jnp.minimum(gates, 0.0) - jnp.log1p(jnp.exp(-jnp.abs(gates)))
        gl = jnp.where(lane < NH_A, gates, log_sig)
        t_loc = lax.broadcasted_iota(jnp.int32, gl.shape, 0) & (L - 1)
        b = gl
        k = 1
        while k < L:
            b = b + jnp.where(t_loc >= k, pltpu.roll(b, k, axis=0), 0.0)
            k *= 2
        r_col = jnp.where(lane < NH_A, gl, b)
        r_row = r_col.T

        q_all = jnp.dot(xn, wqkvo_ref[:, 0:QK_W], preferred_element_type=F32)
        q_all = (q_all * (DQK_A ** -0.5)).astype(BF16)
        k_all = jnp.dot(xn, wqkvo_ref[:, QK_W:2 * QK_W], preferred_element_type=F32)
        v_all = jnp.dot(xn, wqkvo_ref[:, 2 * QK_W:2 * QK_W + V_W], preferred_element_type=F32).astype(BF16)
        o_all = jnp.dot(xn, wqkvo_ref[:, 2 * QK_W + V_W:2 * QK_W + 2 * V_W], preferred_element_type=F32)
        yield

        row_i = lax.broadcasted_iota(jnp.int32, (L, L), 0)
        col_i = lax.broadcasted_iota(jnp.int32, (L, L), 1)
        causal = col_i <= row_i
        hnorm = hnorm_ref[...]

        def weights(c, h):
            lo = c * L
            rc = r_col[lo:lo + L]
            rr = r_row[:, lo:lo + L]
            qh = q_all[lo:lo + L, h * DQK_A:(h + 1) * DQK_A]
            kh = k_all[lo:lo + L, h * DQK_A:(h + 1) * DQK_A]
            i_col = rc[:, h:h + 1]
            b_col = rc[:, NH_A + h:NH_A + h + 1]
            i_row = rr[h:h + 1, :]
            b_row = rr[NH_A + h:NH_A + h + 1, :]
            b_tot = b_row[:, L - 1:L]
            m_prev = m_ref[h:h + 1, 0:1]

            s = lax.dot_general(qh, kh.astype(BF16), (((1,), (1,)), ((), ())), preferred_element_type=F32)
            d = jnp.where(causal, b_col - b_row + i_row, -jnp.inf)
            inter = b_col + m_prev
            m_t = jnp.maximum(inter, jnp.max(d, axis=-1, keepdims=True))
            sw = s * jnp.exp(d - m_t)
            w_col = b_tot - b_col + i_col
            w_row = b_tot - b_row + i_row
            m_new = jnp.maximum(b_tot + m_prev, jnp.max(w_row, axis=-1, keepdims=True))
            ks = jnp.exp(w_col - m_new) * kh
            m_ref[h:h + 1, :] = jnp.broadcast_to(m_new, (1, LANES_V7X))
            return dict(lo=lo, h=h, qh=qh, m_t=m_t, e_int=jnp.exp(inter - m_t),
                        sw_sum=jnp.sum(sw, axis=-1, keepdims=True), sw=sw.astype(BF16),
                        decay=jnp.exp(b_tot + m_prev - m_new),
                        ks_sum=jnp.sum(ks, axis=0, keepdims=True), ks=ks.astype(BF16))

        def outputs(a):
            lo, h, qh = a["lo"], a["h"], a["qh"]
            vh = v_all[lo:lo + L, h * DV_A:(h + 1) * DV_A]
            n_prev = n_ref[h:h + 1, :]
            ct = ct_ref[h]
            lhs = jnp.concatenate([a["sw"], (a["e_int"] * qh.astype(F32)).astype(BF16)], axis=1)
            num = jnp.dot(lhs, jnp.concatenate([vh, ct.astype(BF16)], axis=0), preferred_element_type=F32)
            ct_ref[h] = a["decay"] * ct + lax.dot_general(
                a["ks"], vh, (((0,), (0,)), ((), ())), preferred_element_type=F32)
            n_ref[h:h + 1, :] = a["decay"] * n_prev + a["ks_sum"]
            qn = jnp.sum(qh.astype(F32) * n_prev, axis=-1, keepdims=True)
            den = a["sw_sum"] + a["e_int"] * qn
            inv = 1.0 / jnp.maximum(jnp.abs(den), jnp.exp(-a["m_t"]))
            scale = inv * lax.rsqrt(inv * inv * jnp.mean(num * num, axis=-1, keepdims=True) + EPS)
            hv = num * scale * hnorm[:, h * DV_A:(h + 1) * DV_A]
            hv = hv * jax.nn.sigmoid(o_all[lo:lo + L, h * DV_A:(h + 1) * DV_A])
            hg_ref[lo:lo + L, h * DV_A:(h + 1) * DV_A] = hv.astype(BF16)

        pending = None
        for c in range(tm // L):
            for h in range(NH_A):
                a = weights(c, h)
                yield
                if pending is not None:
                    outputs(pending)
                    yield
                pending = a
        outputs(pending)

    def post():
        x1 = x_prev_ref[...] + jnp.dot(hg_prev_ref[...], wout_ref[...], preferred_element_type=F32)
        yield
        x2 = yield from _ffn_steps(x1, fnorm_ref[...], wgu_ref, wd_ref, act_ref, fine=True)
        o_ref[...] = x2

    def keep_for_next_step():
        hg_prev_ref[...] = hg_ref[...]
        x_prev_ref[...] = x_ref[...]

    _pipelined_steps(mixer, post, keep_for_next_step)


def _shared_kv_kernel(x_ref, norm_ref, w_ref, b_ref, o_ref):
    xn = _rms(x_ref[...], norm_ref[...]).astype(BF16)
    o_ref[...] = (jnp.dot(xn, w_ref[...], preferred_element_type=F32) + b_ref[...]).astype(BF16)


def _swa_layer_kernel(*refs, tm, final):
    if final:
        (sink_ref, x_ref, kvc_ref, kvp_ref, bnorm_ref, wq_ref, bq_ref, wo_ref, bo_ref, fnorm_ref, wgu_ref, wd_ref,
         final_ref, o_ref, attn_ref, attn_prev_ref, x_prev_ref, act_ref) = refs
    else:
        (sink_ref, x_ref, kvc_ref, kvp_ref, bnorm_ref, wq_ref, bq_ref, wo_ref, bo_ref, fnorm_ref, wgu_ref, wd_ref,
         o_ref, attn_ref, attn_prev_ref, x_prev_ref, act_ref) = refs

    W = WINDOW
    pairs = GRP_B // 2
    half = pairs * W

    kj = lax.broadcasted_iota(jnp.int32, (2 * W, GRP_B * W), 0)
    qi = lax.broadcasted_iota(jnp.int32, (2 * W, GRP_B * W), 1) & (W - 1)
    band = (kj > qi) & (kj <= qi + W)
    band_first = band & ((kj >= W) | (pl.program_id(0) > 0))
    lane_i = lax.broadcasted_iota(jnp.int32, (2 * W, LANES_V7X), 1)
    low = lane_i < DH_B
    zero = jnp.zeros((2 * W, LANES_V7X), BF16)
    ones_lane_hi = jnp.where(lane_i == DH_B, 1.0, 0.0).astype(BF16)
    ones_lane_lo = jnp.where(lane_i == 0, 1.0, 0.0).astype(BF16)
    nt = (((1,), (1,)), ((), ()))
    tn = (((0,), (0,)), ((), ()))

    def mixer():
        xn = _rms(x_ref[...], bnorm_ref[...]).astype(BF16)
        q = jnp.dot(xn, wq_ref[...], preferred_element_type=F32) + bq_ref[...]
        q = (q * (DH_B ** -0.5 * LOG2_E)).astype(BF16)
        for b in range(tm // W):
            k0 = b * W
            qb = q[b * W:(b + 1) * W]
            kv_prev = kvp_ref[...] if k0 == 0 else kvc_ref[k0 - W:k0, :]
            kvb = jnp.concatenate([kv_prev, kvc_ref[k0:k0 + W, :]], axis=0)
            k_01, v_01, k_10, v_10 = (kvb[:, c * LANES_V7X:(c + 1) * LANES_V7X] for c in range(4))
            valid = band_first if k0 == 0 else band
            for kh in range(KVH_B):
                k_same, k_swap = (k_01, k_10) if kh == 0 else (k_10, k_01)
                v_same, v_swap = (v_01, v_10) if kh == 0 else (v_10, v_01)
                k_even, k_odd = jnp.where(low, k_same, zero), jnp.where(low, zero, k_swap)
                v_even = jnp.where(low, v_same, ones_lane_hi)
                v_odd = jnp.where(low, ones_lane_lo, v_swap)
                qp = jnp.concatenate(
                    [qb[:, (kh * pairs + j) * LANES_V7X:(kh * pairs + j + 1) * LANES_V7X] for j in range(pairs)],
                    axis=0)
                s = jnp.concatenate(
                    [lax.dot_general(k_even, qp, nt, preferred_element_type=F32),
                     lax.dot_general(k_odd, qp, nt, preferred_element_type=F32)], axis=1)
                heads = ([kh * GRP_B + 2 * j for j in range(pairs)]
                         + [kh * GRP_B + 2 * j + 1 for j in range(pairs)])
                sk = jnp.concatenate([jnp.full((1, W), sink_ref[hd] * LOG2_E, F32) for hd in heads], axis=1)
                s = jnp.where(valid, s, -jnp.inf)
                mx = jnp.maximum(jnp.max(s, axis=0, keepdims=True), sk)
                p = jnp.exp2(s - mx).astype(BF16)
                sink_p = jnp.exp2(sk - mx)
                yield
                o_even = lax.dot_general(v_even, p[:, :half], tn, preferred_element_type=F32)
                o_odd = lax.dot_general(v_odd, p[:, half:], tn, preferred_element_type=F32)
                inv_even = 1.0 / (o_even[DH_B:DH_B + 1, :] + sink_p[:, :half])
                inv_odd = 1.0 / (o_odd[0:1, :] + sink_p[:, half:])
                o = jnp.concatenate([o_even[:DH_B] * inv_even, o_odd[DH_B:] * inv_odd], axis=0)
                for j in range(pairs):
                    lo = (kh * pairs + j) * LANES_V7X
                    attn_ref[k0:k0 + W, lo:lo + LANES_V7X] = o[:, j * W:(j + 1) * W].T.astype(BF16)

    def post():
        x1 = (x_prev_ref[...] + jnp.dot(attn_prev_ref[...], wo_ref[...], preferred_element_type=F32)
              + bo_ref[...])
        yield
        x2 = yield from _ffn_steps(x1, fnorm_ref[...], wgu_ref, wd_ref, act_ref)
        if final:
            x2 = _rms(x2, final_ref[...])
        o_ref[...] = x2

    def keep_for_next_step():
        attn_prev_ref[...] = attn_ref[...]
        x_prev_ref[...] = x_ref[...]

    _pipelined_steps(mixer, post, keep_for_next_step)


def _resident(operand):
    if isinstance(operand, tuple):
        stack, layer, col0, cols = operand
        assert col0 % cols == 0 and cols % LANES_V7X == 0
        return pl.BlockSpec((None, stack.shape[1], cols), lambda i: (layer, 0, col0 // cols),
                            pipeline_mode=pl.Buffered(1))
    return pl.BlockSpec(operand.shape, lambda i: (0,) * operand.ndim, pipeline_mode=pl.Buffered(1))


def _mlstm_layer(x, anorm, wqkvo, wg, bg, hnorm, wout, fnorm, wgu, wd, *, tm):
    S = x.shape[0]
    n_tiles = S // tm
    operands = [x, anorm, wqkvo, wg, bg, hnorm, wout, fnorm, wgu, wd]
    in_specs = ([pl.BlockSpec((tm, D_MODEL), lambda i: (jnp.minimum(i, n_tiles - 1), 0))]
                + [_resident(a) for a in operands[1:]])
    return pl.pallas_call(
        functools.partial(_mlstm_layer_kernel, tm=tm),
        grid=(n_tiles + 1,),
        in_specs=in_specs,
        out_specs=pl.BlockSpec((tm, D_MODEL), lambda i: (jnp.maximum(i - 1, 0), 0)),
        out_shape=jax.ShapeDtypeStruct((S, D_MODEL), F32),
        scratch_shapes=[
            pltpu.VMEM((NH_A, DQK_A, DV_A), F32),
            pltpu.VMEM((NH_A, DQK_A), F32),
            pltpu.VMEM((NH_A, LANES_V7X), F32),
            pltpu.VMEM((tm, V_W), BF16),
            pltpu.VMEM((tm, V_W), BF16),
            pltpu.VMEM((tm, D_MODEL), F32),
            pltpu.VMEM((tm, D_FF), BF16),
        ],
        compiler_params=pltpu.CompilerParams(
            dimension_semantics=("arbitrary",), vmem_limit_bytes=VMEM_LIMIT_BYTES_V7X),
        name="mlstm_layer",
    )(*[a[0] if isinstance(a, tuple) else a for a in operands])


def _shared_kv(x, norm, w, b, *, tm):
    S = x.shape[0]
    return pl.pallas_call(
        _shared_kv_kernel,
        grid=(S // tm,),
        in_specs=[pl.BlockSpec((tm, D_MODEL), lambda i: (i, 0)), _resident(norm), _resident(w), _resident(b)],
        out_specs=pl.BlockSpec((tm, w.shape[1]), lambda i: (i, 0)),
        out_shape=jax.ShapeDtypeStruct((S, w.shape[1]), BF16),
        compiler_params=pltpu.CompilerParams(
            dimension_semantics=("arbitrary",), vmem_limit_bytes=VMEM_LIMIT_BYTES_V7X),
        name="shared_kv",
    )(x, norm, w, b)


def _swa_layer(x, kv, sinks, bnorm, wq, bq, wo, bo, fnorm, wgu, wd, final_norm, *, tm):
    S = x.shape[0]
    final = final_norm is not None
    n_tiles = S // tm
    blocks_per_tile = tm // WINDOW
    operands = [sinks, x, kv, kv, bnorm, wq, bq, wo, bo, fnorm, wgu, wd]
    if final:
        operands.append(final_norm)
    mixer_tile = lambda i: jnp.minimum(i, n_tiles - 1)
    in_specs = [
        pl.BlockSpec(memory_space=pltpu.SMEM),
        pl.BlockSpec((tm, D_MODEL), lambda i: (mixer_tile(i), 0)),
        pl.BlockSpec((tm, kv.shape[1]), lambda i: (mixer_tile(i), 0)),
        pl.BlockSpec((WINDOW, kv.shape[1]), lambda i: (jnp.maximum(mixer_tile(i) * blocks_per_tile - 1, 0), 0)),
    ] + [_resident(a) for a in operands[4:]]
    return pl.pallas_call(
        functools.partial(_swa_layer_kernel, tm=tm, final=final),
        grid=(n_tiles + 1,),
        in_specs=in_specs,
        out_specs=pl.BlockSpec((tm, D_MODEL), lambda i: (jnp.maximum(i - 1, 0), 0)),
        out_shape=jax.ShapeDtypeStruct((S, D_MODEL), F32),
        scratch_shapes=[
            pltpu.VMEM((tm, NH_B * DH_B), BF16),
            pltpu.VMEM((tm, NH_B * DH_B), BF16),
            pltpu.VMEM((tm, D_MODEL), F32),
            pltpu.VMEM((tm, D_FF), BF16),
        ],
        compiler_params=pltpu.CompilerParams(
            dimension_semantics=("arbitrary",), vmem_limit_bytes=VMEM_LIMIT_BYTES_V7X),
        name="swa_layer_final" if final else "swa_layer",
    )(*[a[0] if isinstance(a, tuple) else a for a in operands])


def kernel(x, a_norm, a_w_in, a_b_gates, a_head_norm, a_w_out, kv_norm, w_kv, b_kv, b_norm, b_w_q, b_b_q,
           b_sinks, b_w_out, b_b_out, ffn_norm, w_gate_up, w_down, final_norm):
    bsz, S, _ = x.shape
    assert bsz == 1 and S % ROW_TILE_A == 0 and S % ROW_TILE_B == 0 and S % ROW_TILE_KV == 0
    n_a = a_w_in.shape[0]
    n_b = b_w_q.shape[0]
    row = lambda v: v.reshape(1, -1).astype(F32)
    n_qkvo = 2 * QK_W + 2 * V_W
    gate_pad = LANES_V7X - 2 * NH_A
    layer_of = lambda stack, l: (stack, l, 0, stack.shape[2])
    w_in = jnp.pad(a_w_in, ((0, 0), (0, 0), (0, gate_pad))).astype(BF16)
    w_out_a, w_q, w_out_b = a_w_out.astype(BF16), b_w_q.astype(BF16), b_w_out.astype(BF16)
    w_gu, w_d = w_gate_up.astype(BF16), w_down.astype(BF16)

    h = x[0]
    for l in range(n_a):
        bg = jnp.pad(a_b_gates[l], (0, gate_pad)).reshape(1, -1).astype(F32)
        h = _mlstm_layer(
            h, row(a_norm[l]), (w_in, l, 0, n_qkvo), (w_in, l, n_qkvo, LANES_V7X), bg, row(a_head_norm[l]),
            layer_of(w_out_a, l), row(ffn_norm[l]), layer_of(w_gu, l), layer_of(w_d, l), tm=ROW_TILE_A)
    kw = KVH_B * DH_B
    order = jnp.array([*range(2 * kw), *range(DH_B, kw), *range(DH_B), *range(kw + DH_B, 2 * kw),
                       *range(kw, kw + DH_B)], jnp.int32)
    kv = _shared_kv(h, row(kv_norm), w_kv[:, order].astype(BF16), row(b_kv[order]), tm=ROW_TILE_KV)
    for j in range(n_b):
        l = n_a + j
        h = _swa_layer(
            h, kv, b_sinks[j].astype(F32), row(b_norm[j]), layer_of(w_q, j), row(b_b_q[j]), layer_of(w_out_b, j),
            row(b_b_out[j]), row(ffn_norm[l]), layer_of(w_gu, l), layer_of(w_d, l),
            row(final_norm) if j == n_b - 1 else None, tm=ROW_TILE_B)
    return h[None]
```

```python
import functools

import jax
import jax.numpy as jnp
from jax import lax
from jax.experimental import pallas as pl
from jax.experimental.pallas import tpu as pltpu

F32 = jnp.float32
BF16 = jnp.bfloat16

D_MODEL = 1024
EPS = 1e-6
NH_A = 4
DV_A = D_MODEL // NH_A
DQK_A = DV_A // 2
GATE_CAP = 15.0
LOG2_E = 1.4426950408889634
QK_W = NH_A * DQK_A
V_W = NH_A * DV_A
NH_B = 16
KVH_B = 2
GRP_B = NH_B // KVH_B
DH_B = 64
WINDOW = 128
D_FF = 2816

LANES_V7X = 128
MXU_DIM_V7X = 256
VMEM_LIMIT_BYTES_V7X = 58 * 1024 * 1024

MLSTM_CHUNK = 128
FFN_CHUNK = MXU_DIM_V7X
ROW_TILE_A = 512
ROW_TILE_B = 512
ROW_TILE_KV = 1024


def _rms(x, g):
    ms = jnp.mean(x * x, axis=-1, keepdims=True)
    return x * lax.rsqrt(ms + EPS) * g


def _ffn_steps(x1, fnorm, wgu_ref, wd_ref, act_ref, fine=False):
    xn = _rms(x1, fnorm).astype(BF16)
    for j in range(D_FF // FFN_CHUNK):
        lo = j * FFN_CHUNK
        g = jnp.dot(xn, wgu_ref[:, lo:lo + FFN_CHUNK], preferred_element_type=F32)
        if fine:
            yield
        u = jnp.dot(xn, wgu_ref[:, D_FF + lo:D_FF + lo + FFN_CHUNK], preferred_element_type=F32)
        act_ref[:, lo:lo + FFN_CHUNK] = (g * jax.nn.sigmoid(g) * u).astype(BF16)
        yield
    return x1 + jnp.dot(act_ref[...], wd_ref[...], preferred_element_type=F32)


def _interleave(main, side):
    live = [main, side]
    while live:
        for gen in list(live):
            try:
                next(gen)
            except StopIteration:
                live.remove(gen)


def _run(gen):
    for _ in gen:
        pass


def _pipelined_steps(mixer, post, keep_for_next_step):
    i = pl.program_id(0)
    last = pl.num_programs(0) - 1

    @pl.when(i == 0)
    def _():
        _run(mixer())

    @pl.when(jnp.logical_and(i > 0, i < last))
    def _():
        _interleave(mixer(), post())

    @pl.when(i == last)
    def _():
        _run(post())

    @pl.when(i < last)
    def _():
        keep_for_next_step()


def _mlstm_layer_kernel(x_ref, anorm_ref, wqkvo_ref, wg_ref, bg_ref, hnorm_ref, wout_ref, fnorm_ref, wgu_ref, wd_ref,
                        o_ref, ct_ref, n_ref, m_ref, hg_ref, hg_prev_ref, x_prev_ref, act_ref, *, tm):
    L = MLSTM_CHUNK

    @pl.when(pl.program_id(0) == 0)
    def _():
        ct_ref[...] = jnp.zeros_like(ct_ref)
        n_ref[...] = jnp.zeros_like(n_ref)
        m_ref[...] = jnp.zeros_like(m_ref)

    def mixer():
        xn = _rms(x_ref[...], anorm_ref[...]).astype(BF16)

        gates = jnp.dot(xn, wg_ref[...], preferred_element_type=F32) + bg_ref[...]
        gates = GATE_CAP * jnp.tanh(gates / GATE_CAP)
        lane = lax.broadcasted_iota(jnp.int32, gates.shape, 1)
        log_sig = jnp.minimum(gates, 0.0) - jnp.log1p(jnp.exp(-jnp.abs(gates)))
        gl = jnp.where(lane < NH_A, gates, log_sig)
        t_loc = lax.broadcasted_iota(jnp.int32, gl.shape, 0) & (L - 1)
        b = gl
        k = 1
        while k < L:
            b = b + jnp.where(t_loc >= k, pltpu.roll(b, k, axis=0), 0.0)
            k *= 2
        r_col = jnp.where(lane < NH_A, gl, b)
        r_row = r_col.T

        q_all = jnp.dot(xn, wqkvo_ref[:, 0:QK_W], preferred_element_type=F32)
        q_all = (q_all * (DQK_A ** -0.5)).astype(BF16)
        k_all = jnp.dot(xn, wqkvo_ref[:, QK_W:2 * QK_W], preferred_element_type=F32)
        v_all = jnp.dot(xn, wqkvo_ref[:, 2 * QK_W:2 * QK_W + V_W], preferred_element_type=F32).astype(BF16)
        o_all = jnp.dot(xn, wqkvo_ref[:, 2 * QK_W + V_W:2 * QK_W + 2 * V_W], preferred_element_type=F32)
        yield

        row_i = lax.broadcasted_iota(jnp.int32, (L, L), 0)
        col_i = lax.broadcasted_iota(jnp.int32, (L, L), 1)
        causal = col_i <= row_i
        hnorm = hnorm_ref[...]

        def weights(c, h):
            lo = c * L
            rc = r_col[lo:lo + L]
            rr = r_row[:, lo:lo + L]
            qh = q_all[lo:lo + L, h * DQK_A:(h + 1) * DQK_A]
            kh = k_all[lo:lo + L, h * DQK_A:(h + 1) * DQK_A]
            i_col = rc[:, h:h + 1]
            b_col = rc[:, NH_A + h:NH_A + h + 1]
            i_row = rr[h:h + 1, :]
            b_row = rr[NH_A + h:NH_A + h + 1, :]
            b_tot = b_row[:, L - 1:L]
            m_prev = m_ref[h:h + 1, 0:1]

            s = lax.dot_general(qh, kh.astype(BF16), (((1,), (1,)), ((), ())), preferred_element_type=F32)
            d = jnp.where(causal, b_col - b_row + i_row, -jnp.inf)
            inter = b_col + m_prev
            m_t = jnp.maximum(inter, jnp.max(d, axis=-1, keepdims=True))
            sw = s * jnp.exp(d - m_t)
            w_col = b_tot - b_col + i_col
            w_row = b_tot - b_row + i_row
            m_new = jnp.maximum(b_tot + m_prev, jnp.max(w_row, axis=-1, keepdims=True))
            ks = jnp.exp(w_col - m_new) * kh
            m_ref[h:h + 1, :] = jnp.broadcast_to(m_new, (1, LANES_V7X))
            return dict(lo=lo, h=h, qh=qh, m_t=m_t, e_int=jnp.exp(inter - m_t),
                        sw_sum=jnp.sum(sw, axis=-1, keepdims=True), sw=sw.astype(BF16),
                        decay=jnp.exp(b_tot + m_prev - m_new),
                        ks_sum=jnp.sum(ks, axis=0, keepdims=True), ks=ks.astype(BF16))

        def outputs(a):
            lo, h, qh = a["lo"], a["h"], a["qh"]
            vh = v_all[lo:lo + L, h * DV_A:(h + 1) * DV_A]
            n_prev = n_ref[h:h + 1, :]
            ct = ct_ref[h]
            lhs = jnp.concatenate([a["sw"], (a["e_int"] * qh.astype(F32)).astype(BF16)], axis=1)
            num = jnp.dot(lhs, jnp.concatenate([vh, ct.astype(BF16)], axis=0), preferred_element_type=F32)
            ct_ref[h] = a["decay"] * ct + lax.dot_general(
                a["ks"], vh, (((0,), (0,)), ((), ())), preferred_element_type=F32)
            n_ref[h:h + 1, :] = a["decay"] * n_prev + a["ks_sum"]
            qn = jnp.sum(qh.astype(F32) * n_prev, axis=-1, keepdims=True)
            den = a["sw_sum"] + a["e_int"] * qn
            inv = 1.0 / jnp.maximum(jnp.abs(den), jnp.exp(-a["m_t"]))
            scale = inv * lax.rsqrt(inv * inv * jnp.mean(num * num, axis=-1, keepdims=True) + EPS)
            hv = num * scale * hnorm[:, h * DV_A:(h + 1) * DV_A]
            hv = hv * jax.nn.sigmoid(o_all[lo:lo + L, h * DV_A:(h + 1) * DV_A])
            hg_ref[lo:lo + L, h * DV_A:(h + 1) * DV_A] = hv.astype(BF16)

        pending = None
        for c in range(tm // L):
            for h in range(NH_A):
                a = weights(c, h)
                yield
                if pending is not None:
                    outputs(pending)
                    yield
                pending = a
        outputs(pending)

    def post():
        x1 = x_prev_ref[...] + jnp.dot(hg_prev_ref[...], wout_ref[...], preferred_element_type=F32)
        yield
        x2 = yield from _ffn_steps(x1, fnorm_ref[...], wgu_ref, wd_ref, act_ref, fine=True)
        o_ref[...] = x2

    def keep_for_next_step():
        hg_prev_ref[...] = hg_ref[...]
        x_prev_ref[...] = x_ref[...]

    _pipelined_steps(mixer, post, keep_for_next_step)


def _shared_kv_kernel(x_ref, norm_ref, w_ref, b_ref, o_ref):
    xn = _rms(x_ref[...], norm_ref[...]).astype(BF16)
    o_ref[...] = (jnp.dot(xn, w_ref[...], preferred_element_type=F32) + b_ref[...]).astype(BF16)


def _swa_layer_kernel(*refs, tm, final):
    if final:
        (sink_ref, x_ref, kvc_ref, kvp_ref, bnorm_ref, wq_ref, bq_ref, wo_ref, bo_ref, fnorm_ref, wgu_ref, wd_ref,
         final_ref, o_ref, attn_ref, attn_prev_ref, x_prev_ref, act_ref) = refs
    else:
        (sink_ref, x_ref, kvc_ref, kvp_ref, bnorm_ref, wq_ref, bq_ref, wo_ref, bo_ref, fnorm_ref, wgu_ref, wd_ref,
         o_ref, attn_ref, attn_prev_ref, x_prev_ref, act_ref) = refs

    W = WINDOW
    pairs = GRP_B // 2
    half = pairs * W

    kj = lax.broadcasted_iota(jnp.int32, (2 * W, GRP_B * W), 0)
    qi = lax.broadcasted_iota(jnp.int32, (2 * W, GRP_B * W), 1) & (W - 1)
    band = (kj > qi) & (kj <= qi + W)
    band_first = band & ((kj >= W) | (pl.program_id(0) > 0))
    lane_i = lax.broadcasted_iota(jnp.int32, (2 * W, LANES_V7X), 1)
    low = lane_i < DH_B
    zero = jnp.zeros((2 * W, LANES_V7X), BF16)
    ones_lane_hi = jnp.where(lane_i == DH_B, 1.0, 0.0).astype(BF16)
    ones_lane_lo = jnp.where(lane_i == 0, 1.0, 0.0).astype(BF16)
    nt = (((1,), (1,)), ((), ()))
    tn = (((0,), (0,)), ((), ()))

    def mixer():
        xn = _rms(x_ref[...], bnorm_ref[...]).astype(BF16)
        q = jnp.dot(xn, wq_ref[...], preferred_element_type=F32) + bq_ref[...]
        q = (q * (DH_B ** -0.5 * LOG2_E)).astype(BF16)
        for b in range(tm // W):
            k0 = b * W
            qb = q[b * W:(b + 1) * W]
            kv_prev = kvp_ref[...] if k0 == 0 else kvc_ref[k0 - W:k0, :]
            kvb = jnp.concatenate([kv_prev, kvc_ref[k0:k0 + W, :]], axis=0)
            k_01, v_01, k_10, v_10 = (kvb[:, c * LANES_V7X:(c + 1) * LANES_V7X] for c in range(4))
            valid = band_first if k0 == 0 else band
            for kh in range(KVH_B):
                k_same, k_swap = (k_01, k_10) if kh == 0 else (k_10, k_01)
                v_same, v_swap = (v_01, v_10) if kh == 0 else (v_10, v_01)
                k_even, k_odd = jnp.where(low, k_same, zero), jnp.where(low, zero, k_swap)
                v_even = jnp.where(low, v_same, ones_lane_hi)
                v_odd = jnp.where(low, ones_lane_lo, v_swap)
                qp = jnp.concatenate(
                    [qb[:, (kh * pairs + j) * LANES_V7X:(kh * pairs + j + 1) * LANES_V7X] for j in range(pairs)],
                    axis=0)
                s = jnp.concatenate(
                    [lax.dot_general(k_even, qp, nt, preferred_element_type=F32),
                     lax.dot_general(k_odd, qp, nt, preferred_element_type=F32)], axis=1)
                heads = ([kh * GRP_B + 2 * j for j in range(pairs)]
                         + [kh * GRP_B + 2 * j + 1 for j in range(pairs)])
                sk = jnp.concatenate([jnp.full((1, W), sink_ref[hd] * LOG2_E, F32) for hd in heads], axis=1)
                s = jnp.where(valid, s, -jnp.inf)
                mx = jnp.maximum(jnp.max(s, axis=0, keepdims=True), sk)
                p = jnp.exp2(s - mx).astype(BF16)
                sink_p = jnp.exp2(sk - mx)
                yield
                o_even = lax.dot_general(v_even, p[:, :half], tn, preferred_element_type=F32)
                o_odd = lax.dot_general(v_odd, p[:, half:], tn, preferred_element_type=F32)
                inv_even = 1.0 / (o_even[DH_B:DH_B + 1, :] + sink_p[:, :half])
                inv_odd = 1.0 / (o_odd[0:1, :] + sink_p[:, half:])
                o = jnp.concatenate([o_even[:DH_B] * inv_even, o_odd[DH_B:] * inv_odd], axis=0)
                for j in range(pairs):
                    lo = (kh * pairs + j) * LANES_V7X
                    attn_ref[k0:k0 + W, lo:lo + LANES_V7X] = o[:, j * W:(j + 1) * W].T.astype(BF16)

    def post():
        x1 = (x_prev_ref[...] + jnp.dot(attn_prev_ref[...], wo_ref[...], preferred_element_type=F32)
              + bo_ref[...])
        yield
        x2 = yield from _ffn_steps(x1, fnorm_ref[...], wgu_ref, wd_ref, act_ref)
        if final:
            x2 = _rms(x2, final_ref[...])
        o_ref[...] = x2

    def keep_for_next_step():
        attn_prev_ref[...] = attn_ref[...]
        x_prev_ref[...] = x_ref[...]

    _pipelined_steps(mixer, post, keep_for_next_step)


def _resident(operand):
    if isinstance(operand, tuple):
        stack, layer = operand
        cols = stack.shape[2] // LANES_V7X * LANES_V7X
        return pl.BlockSpec((None, stack.shape[1], cols), lambda i: (layer, 0, 0), pipeline_mode=pl.Buffered(1))
    return pl.BlockSpec(operand.shape, lambda i: (0,) * operand.ndim, pipeline_mode=pl.Buffered(1))


def _mlstm_layer(x, anorm, wqkvo, wg, bg, hnorm, wout, fnorm, wgu, wd, *, tm):
    S = x.shape[0]
    n_tiles = S // tm
    operands = [x, anorm, wqkvo, wg, bg, hnorm, wout, fnorm, wgu, wd]
    in_specs = ([pl.BlockSpec((tm, D_MODEL), lambda i: (jnp.minimum(i, n_tiles - 1), 0))]
                + [_resident(a) for a in operands[1:]])
    return pl.pallas_call(
        functools.partial(_mlstm_layer_kernel, tm=tm),
        grid=(n_tiles + 1,),
        in_specs=in_specs,
        out_specs=pl.BlockSpec((tm, D_MODEL), lambda i: (jnp.maximum(i - 1, 0), 0)),
        out_shape=jax.ShapeDtypeStruct((S, D_MODEL), F32),
        scratch_shapes=[
            pltpu.VMEM((NH_A, DQK_A, DV_A), F32),
            pltpu.VMEM((NH_A, DQK_A), F32),
            pltpu.VMEM((NH_A, LANES_V7X), F32),
            pltpu.VMEM((tm, V_W), BF16),
            pltpu.VMEM((tm, V_W), BF16),
            pltpu.VMEM((tm, D_MODEL), F32),
            pltpu.VMEM((tm, D_FF), BF16),
        ],
        compiler_params=pltpu.CompilerParams(
            dimension_semantics=("arbitrary",), vmem_limit_bytes=VMEM_LIMIT_BYTES_V7X),
        name="mlstm_layer",
    )(*[a[0] if isinstance(a, tuple) else a for a in operands])


def _shared_kv(x, norm, w, b, *, tm):
    S = x.shape[0]
    return pl.pallas_call(
        _shared_kv_kernel,
        grid=(S // tm,),
        in_specs=[pl.BlockSpec((tm, D_MODEL), lambda i: (i, 0)), _resident(norm), _resident(w), _resident(b)],
        out_specs=pl.BlockSpec((tm, w.shape[1]), lambda i: (i, 0)),
        out_shape=jax.ShapeDtypeStruct((S, w.shape[1]), BF16),
        compiler_params=pltpu.CompilerParams(
            dimension_semantics=("arbitrary",), vmem_limit_bytes=VMEM_LIMIT_BYTES_V7X),
        name="shared_kv",
    )(x, norm, w, b)


def _swa_layer(x, kv, sinks, bnorm, wq, bq, wo, bo, fnorm, wgu, wd, final_norm, *, tm):
    S = x.shape[0]
    final = final_norm is not None
    n_tiles = S // tm
    blocks_per_tile = tm // WINDOW
    operands = [sinks, x, kv, kv, bnorm, wq, bq, wo, bo, fnorm, wgu, wd]
    if final:
        operands.append(final_norm)
    mixer_tile = lambda i: jnp.minimum(i, n_tiles - 1)
    in_specs = [
        pl.BlockSpec(memory_space=pltpu.SMEM),
        pl.BlockSpec((tm, D_MODEL), lambda i: (mixer_tile(i), 0)),
        pl.BlockSpec((tm, kv.shape[1]), lambda i: (mixer_tile(i), 0)),
        pl.BlockSpec((WINDOW, kv.shape[1]), lambda i: (jnp.maximum(mixer_tile(i) * blocks_per_tile - 1, 0), 0)),
    ] + [_resident(a) for a in operands[4:]]
    return pl.pallas_call(
        functools.partial(_swa_layer_kernel, tm=tm, final=final),
        grid=(n_tiles + 1,),
        in_specs=in_specs,
        out_specs=pl.BlockSpec((tm, D_MODEL), lambda i: (jnp.maximum(i - 1, 0), 0)),
        out_shape=jax.ShapeDtypeStruct((S, D_MODEL), F32),
        scratch_shapes=[
            pltpu.VMEM((tm, NH_B * DH_B), BF16),
            pltpu.VMEM((tm, NH_B * DH_B), BF16),
            pltpu.VMEM((tm, D_MODEL), F32),
            pltpu.VMEM((tm, D_FF), BF16),
        ],
        compiler_params=pltpu.CompilerParams(
            dimension_semantics=("arbitrary",), vmem_limit_bytes=VMEM_LIMIT_BYTES_V7X),
        name="swa_layer_final" if final else "swa_layer",
    )(*[a[0] if isinstance(a, tuple) else a for a in operands])


def kernel(x, a_norm, a_w_in, a_b_gates, a_head_norm, a_w_out, kv_norm, w_kv, b_kv, b_norm, b_w_q, b_b_q,
           b_sinks, b_w_out, b_b_out, ffn_norm, w_gate_up, w_down, final_norm):
    bsz, S, _ = x.shape
    assert bsz == 1 and S % ROW_TILE_A == 0 and S % ROW_TILE_B == 0 and S % ROW_TILE_KV == 0
    n_a = a_w_in.shape[0]
    n_b = b_w_q.shape[0]
    row = lambda v: v.reshape(1, -1).astype(F32)
    n_qkvo = 2 * QK_W + 2 * V_W
    gate_pad = LANES_V7X - 2 * NH_A
    w_in, w_out_a = a_w_in.astype(BF16), a_w_out.astype(BF16)
    w_q, w_out_b = b_w_q.astype(BF16), b_w_out.astype(BF16)
    w_gu, w_d = w_gate_up.astype(BF16), w_down.astype(BF16)

    h = x[0]
    for l in range(n_a):
        wg = jnp.pad(a_w_in[l][:, n_qkvo:], ((0, 0), (0, gate_pad))).astype(BF16)
        bg = jnp.pad(a_b_gates[l], (0, gate_pad)).reshape(1, -1).astype(F32)
        h = _mlstm_layer(
            h, row(a_norm[l]), (w_in, l), wg, bg, row(a_head_norm[l]), (w_out_a, l), row(ffn_norm[l]),
            (w_gu, l), (w_d, l), tm=ROW_TILE_A)
    kw = KVH_B * DH_B
    order = jnp.array([*range(2 * kw), *range(DH_B, kw), *range(DH_B), *range(kw + DH_B, 2 * kw),
                       *range(kw, kw + DH_B)], jnp.int32)
    kv = _shared_kv(h, row(kv_norm), w_kv[:, order].astype(BF16), row(b_kv[order]), tm=ROW_TILE_KV)
    for j in range(n_b):
        l = n_a + j
        h = _swa_layer(
            h, kv, b_sinks[j].astype(F32), row(b_norm[j]), (w_q, j), row(b_b_q[j]), (w_out_b, j),
            row(b_b_out[j]), row(ffn_norm[l]), (w_gu, l), (w_d, l),
            row(final_norm) if j == n_b - 1 else None, tm=ROW_TILE_B)
    return h[None]
```
